```python
import jax, jax.numpy as jnp
from jax import lax
import numpy as np

D_MODEL = 2048
BATCH = 4
SEQ = 4096
DEPTH = 2

CHUNK = 64
D_RNN = 2048
LRU_BLOCKS = 8
LRU_BLOCK_W = D_RNN // LRU_BLOCKS
LRU_C = 8.0
CONV_A_WIDTH = 4
D_CONV = 2048
CONV_B_WIDTH = 3
D_FF = 4 * D_MODEL
EPS = 1e-6

SPLIT_SIZES = (D_RNN, D_RNN, D_CONV, D_CONV, D_CONV, D_MODEL, D_MODEL)
SPLIT_POINTS = tuple(int(v) for v in np.cumsum(SPLIT_SIZES)[:-1])
N_IN = int(sum(SPLIT_SIZES))

kernel_name = "hybrid_rglru_shortconv_gated_trunk"


def _rmsnorm(x, g):
    xf = x.astype(jnp.float32)
    y = xf * lax.rsqrt(jnp.mean(xf * xf, axis=-1, keepdims=True) + EPS)
    return (y * g.astype(jnp.float32)).astype(x.dtype)


def _causal_dwconv(x, w):
    k, c = w.shape
    return lax.conv_general_dilated(
        x, w[:, None, :].astype(x.dtype), window_strides=(1,), padding=[(k - 1, 0)],
        dimension_numbers=("NWC", "WIO", "NWC"), feature_group_count=c)


def _lru_combine(left, right):
    a_l, b_l = left
    a_r, b_r = right
    return a_l * a_r, a_r * b_l + b_r


def _rg_lru(x, wr, br, wi, bi, lam):
    bsz, s, c = x.shape
    xb = x.reshape(bsz, s, LRU_BLOCKS, LRU_BLOCK_W)
    r = jax.nn.sigmoid(jnp.einsum("bsnh,nhk->bsnk", xb, wr) + br).reshape(bsz, s, c)
    i = jax.nn.sigmoid(jnp.einsum("bsnh,nhk->bsnk", xb, wi) + bi).reshape(bsz, s, c)
    log_a = -LRU_C * r.astype(jnp.float32) * jax.nn.softplus(-lam.astype(jnp.float32))
    a = jnp.exp(log_a)
    mult = jnp.sqrt(-jnp.expm1(2.0 * log_a))
    b = mult * (i * x).astype(jnp.float32)
    _, h = lax.associative_scan(_lru_combine, (a, b), axis=1)
    return h.astype(x.dtype)


def _layer(x, g1, w_in, b_in, conv_a_w, conv_a_b, lru_wr, lru_br, lru_wi, lru_bi, lru_lam,
           conv_b_w, w_pa, w_pb, w_o, g2, w_mlp1, w_mlp2):
    h = _rmsnorm(x, g1)
    z = jnp.einsum("bsd,dn->bsn", h, w_in) + b_in
    xa, ya, cb, cc, cx, ga, gb = jnp.split(z, SPLIT_POINTS, axis=-1)
    xa = _causal_dwconv(xa, conv_a_w) + conv_a_b
    xa = _rg_lru(xa, lru_wr, lru_br, lru_wi, lru_bi, lru_lam)
    out_a = jnp.einsum("bsc,cd->bsd", xa * jax.nn.gelu(ya), w_pa)
    out_b = jnp.einsum("bsc,cd->bsd", cb * _causal_dwconv(cc * cx, conv_b_w), w_pb)
    merged = jax.nn.sigmoid(ga) * out_a + jax.nn.sigmoid(gb) * out_b
    x = x + jnp.einsum("bsd,de->bse", merged, w_o)
    h2 = _rmsnorm(x, g2)
    u = jnp.square(jax.nn.relu(jnp.einsum("bsd,df->bsf", h2, w_mlp1)))
    return x + jnp.einsum("bsf,fd->bsd", u, w_mlp2)


def setup_inputs(seed: int = 0) -> dict:
    key = jax.random.key(seed)
    ks = jax.random.split(key, 24)
    f32 = jnp.float32
    L = DEPTH

    def nrm(k, shape, scale):
        return jax.random.normal(k, shape, f32) * scale

    u = jax.random.uniform(ks[10], (L, D_RNN), f32, minval=0.9, maxval=0.999)
    p = u ** (1.0 / LRU_C)
    lru_lam = jnp.log(p) - jnp.log1p(-p)
    return {
        "x": nrm(ks[0], (BATCH, SEQ, D_MODEL), 1.0),
        "norm1_g": 1.0 + nrm(ks[1], (L, D_MODEL), 0.02),
        "w_in": nrm(ks[2], (L, D_MODEL, N_IN), D_MODEL ** -0.5),
        "b_in": nrm(ks[3], (L, N_IN), 0.02),
        "conv_a_w": nrm(ks[4], (L, CONV_A_WIDTH, D_RNN), CONV_A_WIDTH ** -0.5),
        "conv_a_b": nrm(ks[5], (L, D_RNN), 0.02),
        "lru_wr": nrm(ks[6], (L, LRU_BLOCKS, LRU_BLOCK_W, LRU_BLOCK_W), LRU_BLOCK_W ** -0.5),
        "lru_br": nrm(ks[7], (L, LRU_BLOCKS, LRU_BLOCK_W), 0.02),
        "lru_wi": nrm(ks[8], (L, LRU_BLOCKS, LRU_BLOCK_W, LRU_BLOCK_W), LRU_BLOCK_W ** -0.5),
        "lru_bi": nrm(ks[9], (L, LRU_BLOCKS, LRU_BLOCK_W), 0.02),
        "lru_lam": lru_lam,
        "conv_b_w": nrm(ks[11], (L, CONV_B_WIDTH, D_CONV), CONV_B_WIDTH ** -0.5),
        "w_pa": nrm(ks[12], (L, D_RNN, D_MODEL), D_RNN ** -0.5),
        "w_pb": nrm(ks[13], (L, D_CONV, D_MODEL), D_CONV ** -0.5),
        "w_o": nrm(ks[14], (L, D_MODEL, D_MODEL), D_MODEL ** -0.5),
        "norm2_g": 1.0 + nrm(ks[15], (L, D_MODEL), 0.02),
        "w_mlp1": nrm(ks[16], (L, D_MODEL, D_FF), D_MODEL ** -0.5),
        "w_mlp2": nrm(ks[17], (L, D_FF, D_MODEL), D_FF ** -0.5),
        "final_g": 1.0 + nrm(ks[18], (D_MODEL,), 0.02),
    }


def reference(x, norm1_g, w_in, b_in, conv_a_w, conv_a_b, lru_wr, lru_br, lru_wi, lru_bi,
              lru_lam, conv_b_w, w_pa, w_pb, w_o, norm2_g, w_mlp1, w_mlp2, final_g):
    for l in range(DEPTH):
        x = _layer(x, norm1_g[l], w_in[l], b_in[l], conv_a_w[l], conv_a_b[l], lru_wr[l], lru_br[l],
                   lru_wi[l], lru_bi[l], lru_lam[l], conv_b_w[l], w_pa[l], w_pb[l], w_o[l],
                   norm2_g[l], w_mlp1[l], w_mlp2[l])
    return _rmsnorm(x, final_g)
```

```python
import functools

import jax
import jax.numpy as jnp
from jax import lax
from jax.experimental import pallas as pl
from jax.experimental.pallas import tpu as pltpu

D = 2048
LRU_BLOCK_W = 256
LRU_C = 8.0
EPS = 1e-6
TM = 512
CB = 512
ROWS = 64
SUBLANES = 8
VMEM_LIMIT = 60 * 1024 * 1024

XA, YA, CC, CX, CBG, PA, GA, PB, GB, WO = range(10)
N_MIX = 10
N_MLP = 8


def _rmsnorm_rows(src_ref, g_ref, dst_ref, tm):
    def body(s, carry):
        r0 = pl.multiple_of(s * ROWS, ROWS)
        xs = src_ref[pl.ds(r0, ROWS), :]
        ms = jnp.mean(xs * xs, axis=-1, keepdims=True)
        y = xs * lax.rsqrt(ms + EPS) * g_ref[...]
        dst_ref[pl.ds(r0, ROWS), :] = y.astype(dst_ref.dtype)
        return carry

    lax.fori_loop(0, tm // ROWS, body, 0)


def _col_loop(fn):
    def body(nb, carry):
        fn(pl.multiple_of(nb * CB, CB))
        return carry

    lax.fori_loop(0, D // CB, body, 0)


def _mixer_kernel(x_ref, g1_ref, w_ref, bias_ref, caw_ref, cab_ref, wr_ref, wi_ref,
                  br_ref, bi_ref, lam_ref, cbw_ref, o_ref,
                  hn, zs, sa, sb, abf, bbf, carry_xa, carry_p, carry_h, *, tiles_per_seq):
    i = pl.program_id(0)
    j = pl.program_id(1)
    tm = x_ref.shape[0]
    f32 = jnp.float32
    bf16 = jnp.bfloat16

    def proj(lhs_ref, col):
        return jnp.dot(lhs_ref[...], w_ref[:, pl.ds(col, CB)], preferred_element_type=f32)

    def bias(row, col):
        return bias_ref[row:row + 1, pl.ds(col, CB)]

    @pl.when(jnp.logical_and(j == XA, i % tiles_per_seq == 0))
    def _():
        carry_xa[...] = jnp.zeros_like(carry_xa)
        carry_p[...] = jnp.zeros_like(carry_p)
        carry_h[...] = jnp.zeros_like(carry_h)

    @pl.when(j == XA)
    def _():
        _rmsnorm_rows(x_ref, g1_ref, hn, tm)
        zs[0:SUBLANES, :] = carry_xa[...]

        def blk(col):
            cs = pl.ds(col, CB)
            zs[pl.ds(SUBLANES, tm), cs] = proj(hn, col) + bias(XA, col)
            xc = (caw_ref[3:4, cs] * zs[pl.ds(SUBLANES, tm), cs]
                  + caw_ref[2:3, cs] * zs[pl.ds(SUBLANES - 1, tm), cs]
                  + caw_ref[1:2, cs] * zs[pl.ds(SUBLANES - 2, tm), cs]
                  + caw_ref[0:1, cs] * zs[pl.ds(SUBLANES - 3, tm), cs]
                  + cab_ref[:, cs])
            xcb = xc.astype(bf16)
            for q in range(CB // LRU_BLOCK_W):
                qs = slice(q * LRU_BLOCK_W, (q + 1) * LRU_BLOCK_W)
                gcol = pl.ds(col + q * LRU_BLOCK_W, LRU_BLOCK_W)
                blk_id = lax.shift_right_logical(col, LRU_BLOCK_W.bit_length() - 1) + q
                r = jax.nn.sigmoid(
                    jnp.dot(xcb[:, qs], wr_ref[blk_id], preferred_element_type=f32) + br_ref[:, gcol])
                ig = jax.nn.sigmoid(
                    jnp.dot(xcb[:, qs], wi_ref[blk_id], preferred_element_type=f32) + bi_ref[:, gcol])
                log_a = (-LRU_C * jax.nn.softplus(-lam_ref[:, gcol])) * r
                a = jnp.exp(log_a)
                mult = jnp.sqrt((1.0 - a) * (1.0 + a))
                sa[:, gcol] = a
                sb[:, gcol] = mult * (ig * xc[:, qs])

        _col_loop(blk)
        carry_xa[...] = zs[pl.ds(tm, SUBLANES), :]

        def scan_body(g, h):
            r0 = pl.multiple_of(g * SUBLANES, SUBLANES)
            a = sa[pl.ds(r0, SUBLANES), :]
            b = sb[pl.ds(r0, SUBLANES), :]
            row = lax.broadcasted_iota(jnp.int32, a.shape, 0)
            for d in (1, 2, 4):
                keep = row >= d
                b = a * jnp.where(keep, pltpu.roll(b, d, 0), 0.0) + b
                a = a * jnp.where(keep, pltpu.roll(a, d, 0), 1.0)
            hh = a * h + b
            sa[pl.ds(r0, SUBLANES), :] = hh
            return hh[SUBLANES - 1:SUBLANES, :]

        h_last = lax.fori_loop(0, tm // SUBLANES, scan_body, carry_h[0:1, :])
        carry_h[0:1, :] = h_last

    @pl.when(j == YA)
    def _():
        def blk(col):
            cs = pl.ds(col, CB)
            ya = proj(hn, col) + bias(YA, col)
            abf[:, cs] = (sa[:, cs] * jax.nn.gelu(ya)).astype(bf16)

        _col_loop(blk)

    @pl.when(j == CC)
    def _():
        def blk(col):
            sb[:, pl.ds(col, CB)] = proj(hn, col) + bias(CC, col)

        _col_loop(blk)

    @pl.when(j == CX)
    def _():
        zs[0:SUBLANES, :] = carry_p[...]

        def blk(col):
            cs = pl.ds(col, CB)
            zs[pl.ds(SUBLANES, tm), cs] = sb[:, cs] * (proj(hn, col) + bias(CX, col))
            sa[:, cs] = (cbw_ref[2:3, cs] * zs[pl.ds(SUBLANES, tm), cs]
                         + cbw_ref[1:2, cs] * zs[pl.ds(SUBLANES - 1, tm), cs]
                         + cbw_ref[0:1, cs] * zs[pl.ds(SUBLANES - 2, tm), cs])

        _col_loop(blk)
        carry_p[...] = zs[pl.ds(tm, SUBLANES), :]

    @pl.when(j == CBG)
    def _():
        def blk(col):
            cs = pl.ds(col, CB)
            bbf[:, cs] = ((proj(hn, col) + bias(CBG, col)) * sa[:, cs]).astype(bf16)

        _col_loop(blk)

    @pl.when(j == PA)
    def _():
        def blk(col):
            sa[:, pl.ds(col, CB)] = proj(abf, col)

        _col_loop(blk)

    @pl.when(j == GA)
    def _():
        def blk(col):
            cs = pl.ds(col, CB)
            sb[:, cs] = jax.nn.sigmoid(proj(hn, col) + bias(GA - 1, col)) * sa[:, cs]

        _col_loop(blk)

    @pl.when(j == PB)
    def _():
        def blk(col):
            sa[:, pl.ds(col, CB)] = proj(bbf, col)

        _col_loop(blk)

    @pl.when(j == GB)
    def _():
        def blk(col):
            cs = pl.ds(col, CB)
            gate = jax.nn.sigmoid(proj(hn, col) + bias(GB - 2, col))
            abf[:, cs] = (sb[:, cs] + gate * sa[:, cs]).astype(bf16)

        _col_loop(blk)

    @pl.when(j == WO)
    def _():
        def blk(col):
            cs = pl.ds(col, CB)
            o_ref[:, cs] = x_ref[:, cs] + proj(abf, col)

        _col_loop(blk)


def _mlp_kernel(x_ref, g2_ref, w_ref, gf_ref, o_ref, hn, ubf, *, final_norm):
    j = pl.program_id(1)
    tm = x_ref.shape[0]
    f32 = jnp.float32

    @pl.when(j == 0)
    def _():
        _rmsnorm_rows(x_ref, g2_ref, hn, tm)
        o_ref[...] = x_ref[...]

    @pl.when(j % 2 == 0)
    def _():
        def blk(col):
            u = jnp.dot(hn[...], w_ref[:, pl.ds(col, CB)], preferred_element_type=f32)
            u = jnp.maximum(u, 0.0)
            ubf[:, pl.ds(col, CB)] = (u * u).astype(ubf.dtype)

        _col_loop(blk)

    @pl.when(j % 2 == 1)
    def _():
        def blk(col):
            cs = pl.ds(col, CB)
            o_ref[:, cs] += jnp.dot(ubf[...], w_ref[:, cs], preferred_element_type=f32)

        _col_loop(blk)

    if final_norm:
        @pl.when(j == N_MLP - 1)
        def _():
            _rmsnorm_rows(o_ref, gf_ref, o_ref, tm)


def _row(v):
    return v.reshape(1, -1).astype(jnp.float32)


def _mixer(xf, g1, wmix, bmix, caw, cab, wr, wi, br, bi, lam, cbw, *, tiles_per_seq):
    t = xf.shape[0]
    full = lambda shape: pl.BlockSpec(shape, lambda i, j: (0,) * len(shape))
    return pl.pallas_call(
        functools.partial(_mixer_kernel, tiles_per_seq=tiles_per_seq),
        grid=(t // TM, N_MIX),
        in_specs=[
            pl.BlockSpec((TM, D), lambda i, j: (i, 0)),
            full((1, D)),
            pl.BlockSpec((None, D, D), lambda i, j: (j, 0, 0)),
            full(bmix.shape),
            full(caw.shape),
            full((1, D)),
            full(wr.shape),
            full(wi.shape),
            full((1, D)),
            full((1, D)),
            full((1, D)),
            full(cbw.shape),
        ],
        out_specs=pl.BlockSpec((TM, D), lambda i, j: (i, 0)),
        out_shape=jax.ShapeDtypeStruct((t, D), jnp.float32),
        scratch_shapes=[
            pltpu.VMEM((TM, D), jnp.bfloat16),
            pltpu.VMEM((TM + SUBLANES, D), jnp.float32),
            pltpu.VMEM((TM, D), jnp.float32),
            pltpu.VMEM((TM, D), jnp.float32),
            pltpu.VMEM((TM, D), jnp.bfloat16),
            pltpu.VMEM((TM, D), jnp.bfloat16),
            pltpu.VMEM((SUBLANES, D), jnp.float32),
            pltpu.VMEM((SUBLANES, D), jnp.float32),
            pltpu.VMEM((SUBLANES, D), jnp.float32),
        ],
        compiler_params=pltpu.CompilerParams(
            dimension_semantics=("arbitrary", "arbitrary"), vmem_limit_bytes=VMEM_LIMIT),
        name="mixer",
    )(xf, g1, wmix, bmix, caw, cab, wr, wi, br, bi, lam, cbw)


def _mlp(xf, g2, wmlp, gf, *, final_norm):
    t = xf.shape[0]
    return pl.pallas_call(
        functools.partial(_mlp_kernel, final_norm=final_norm),
        grid=(t // TM, N_MLP),
        in_specs=[
            pl.BlockSpec((TM, D), lambda i, j: (i, 0)),
            pl.BlockSpec((1, D), lambda i, j: (0, 0)),
            pl.BlockSpec((None, D, D), lambda i, j: (j, 0, 0)),
            pl.BlockSpec((1, D), lambda i, j: (0, 0)),
        ],
        out_specs=pl.BlockSpec((TM, D), lambda i, j: (i, 0)),
        out_shape=jax.ShapeDtypeStruct((t, D), jnp.float32),
        scratch_shapes=[
            pltpu.VMEM((TM, D), jnp.bfloat16),
            pltpu.VMEM((TM, D), jnp.bfloat16),
        ],
        compiler_params=pltpu.CompilerParams(
            dimension_semantics=("arbitrary", "arbitrary"), vmem_limit_bytes=VMEM_LIMIT),
        name="mlp",
    )(xf, g2, wmlp, gf)


def kernel(x, norm1_g, w_in, b_in, conv_a_w, conv_a_b, lru_wr, lru_br, lru_wi, lru_bi, lru_lam, conv_b_w, w_pa, w_pb, w_o, norm2_g, w_mlp1, w_mlp2, final_g):
    bsz, seq, d = x.shape
    depth = w_in.shape[0]
    assert d == D and seq % TM == 0
    bf16 = jnp.bfloat16
    xf = x.reshape(bsz * seq, d)
    n_ff = w_mlp1.shape[-1] // D
    assert 2 * n_ff == N_MLP
    for l in range(depth):
        seg = lambda k: w_in[l][:, k * D:(k + 1) * D]
        bseg = lambda k: b_in[l][k * D:(k + 1) * D]
        wmix = jnp.stack([seg(0), seg(1), seg(3), seg(4), seg(2), w_pa[l], seg(5), w_pb[l],
                          seg(6), w_o[l]]).astype(bf16)
        bmix = jnp.stack([bseg(0), bseg(1), bseg(3), bseg(4), bseg(2), bseg(5), bseg(6),
                          jnp.zeros((D,), jnp.float32)]).astype(jnp.float32)
        xf = _mixer(
            xf, _row(norm1_g[l]), wmix, bmix, conv_a_w[l], _row(conv_a_b[l]),
            lru_wr[l].astype(bf16), lru_wi[l].astype(bf16), _row(lru_br[l]), _row(lru_bi[l]),
            _row(lru_lam[l]), conv_b_w[l], tiles_per_seq=seq // TM)
        chunks = []
        for c in range(n_ff):
            chunks += [w_mlp1[l][:, c * D:(c + 1) * D], w_mlp2[l][c * D:(c + 1) * D, :]]
        wmlp = jnp.stack(chunks).astype(bf16)
        xf = _mlp(xf, _row(norm2_g[l]), wmlp, _row(final_g), final_norm=(l == depth - 1))
    return xf.reshape(bsz, seq, d)
```

```python
import functools
import math

import jax
import jax.numpy as jnp
from jax import lax
from jax.experimental import pallas as pl
from jax.experimental.pallas import tpu as pltpu

D = 2048
LRU_BLOCK_W = 256
LRU_C = 8.0
EPS = 1e-6
TM = 512
CB = 512
ROWS = 64
SUB = 8
SEG = TM // SUB
CONV_A_TAPS = 4
CONV_B_TAPS = 3
VMEM_LIMIT = 60 * 1024 * 1024

XA, YA, CC, CX, CBG, PA, GA, PB, GB, WO = range(10)
N_MIX = 10
N_MLP = 8


def _rmsnorm_rows(src_ref, g_ref, dst_ref, tm):
    def body(s, carry):
        r0 = pl.multiple_of(s * ROWS, ROWS)
        xs = src_ref[pl.ds(r0, ROWS), :]
        ms = jnp.mean(xs * xs, axis=-1, keepdims=True)
        y = xs * lax.rsqrt(ms + EPS) * g_ref[...]
        dst_ref[pl.ds(r0, ROWS), :] = y.astype(dst_ref.dtype)
        return carry

    lax.fori_loop(0, tm // ROWS, body, 0)


def _sigmoid(v):
    return 0.5 * jnp.tanh(0.5 * v) + 0.5


def _prev_tokens(z, tail_prev, n):
    tm = z.shape[0]
    row = lax.broadcasted_iota(jnp.int32, (SUB, z.shape[1]), 0)
    wrapped = []
    for v in range(n):
        lo = tm - (n - v) * SUB
        cur = z[lo:lo + SUB]
        prev = tail_prev[v * SUB:(v + 1) * SUB]
        wrapped.append(pltpu.roll(jnp.where(row == SUB - 1, prev, cur), 1, 0))
    return [jnp.concatenate(wrapped[n - k:] + [z[:tm - k * SUB]], axis=0) for k in range(1, n + 1)]


def _mixer_kernel(x_ref, g1_ref, w_ref, bias_ref, caw_ref, cab_ref, wr_ref, wi_ref,
                  br_ref, bi_ref, lam_ref, cbw_ref, o_ref,
                  hn, sa, sb, abf, bbf, carry_xa, carry_p, carry_h, *, tiles_per_seq):
    i = pl.program_id(0)
    j = pl.program_id(1)
    tm = x_ref.shape[0]
    f32 = jnp.float32
    bf16 = jnp.bfloat16
    col_blocks = [slice(c, c + CB) for c in range(0, D, CB)]

    def proj(lhs_ref, cs):
        return jnp.dot(lhs_ref[...], w_ref[:, cs], preferred_element_type=f32)

    @pl.when(jnp.logical_and(j == XA, i % tiles_per_seq == 0))
    def _():
        carry_xa[...] = jnp.zeros_like(carry_xa)
        carry_p[...] = jnp.zeros_like(carry_p)
        carry_h[...] = jnp.zeros_like(carry_h)

    @pl.when(j == XA)
    def _():
        _rmsnorm_rows(x_ref, g1_ref, hn, tm)
        z_ahead = proj(hn, col_blocks[0])
        for k, cs in enumerate(col_blocks):
            z = z_ahead + bias_ref[XA:XA + 1, cs]
            if k + 1 < len(col_blocks):
                z_ahead = proj(hn, col_blocks[k + 1])
            n = CONV_A_TAPS - 1
            z1, z2, z3 = _prev_tokens(z, carry_xa[:, cs], n)
            carry_xa[:, cs] = z[tm - n * SUB:]
            xc = (caw_ref[3:4, cs] * z + caw_ref[2:3, cs] * z1 + caw_ref[1:2, cs] * z2
                  + caw_ref[0:1, cs] * z3 + cab_ref[:, cs])
            xcb = xc.astype(bf16)
            for q in range(CB // LRU_BLOCK_W):
                qs = slice(q * LRU_BLOCK_W, (q + 1) * LRU_BLOCK_W)
                gs = slice(cs.start + q * LRU_BLOCK_W, cs.start + (q + 1) * LRU_BLOCK_W)
                blk_id = gs.start // LRU_BLOCK_W
                r_pre = jnp.dot(xcb[:, qs], wr_ref[blk_id], preferred_element_type=f32) + br_ref[:, gs]
                i_pre = jnp.dot(xcb[:, qs], wi_ref[blk_id], preferred_element_type=f32) + bi_ref[:, gs]
                half_c = (-0.5 * LRU_C * math.log2(math.e)) * jax.nn.softplus(-lam_ref[:, gs])
                a = jnp.exp2(half_c * jnp.tanh(0.5 * r_pre) + half_c)
                v = (1.0 - a) * (1.0 + a)
                mult = jnp.where(v > 0.0, v * lax.rsqrt(v), 0.0)
                b = mult * (_sigmoid(i_pre) * xc[:, qs])
                h = b[0:SUB]
                p = a[0:SUB]
                hs, ps = [h], [p]
                for u in range(1, SEG):
                    au = a[u * SUB:(u + 1) * SUB]
                    h = au * h + b[u * SUB:(u + 1) * SUB]
                    p = au * p
                    hs.append(h)
                    ps.append(p)
                c = carry_h[0:1, gs]
                enter = []
                for s in range(SUB):
                    enter.append(c)
                    c = h[s:s + 1] + p[s:s + 1] * c
                carry_h[0:1, gs] = c
                enter = jnp.concatenate(enter, axis=0)
                sa[:, gs] = jnp.concatenate([hu + pu * enter for hu, pu in zip(hs, ps)], axis=0)

    @pl.when(j == YA)
    def _():
        for cs in col_blocks:
            ya = proj(hn, cs) + bias_ref[YA:YA + 1, cs]
            abf[:, cs] = (sa[:, cs] * jax.nn.gelu(ya)).astype(bf16)

    @pl.when(j == CC)
    def _():
        for cs in col_blocks:
            sb[:, cs] = proj(hn, cs) + bias_ref[CC:CC + 1, cs]

    @pl.when(j == CX)
    def _():
        for cs in col_blocks:
            p = sb[:, cs] * (proj(hn, cs) + bias_ref[CX:CX + 1, cs])
            n = CONV_B_TAPS - 1
            p1, p2 = _prev_tokens(p, carry_p[:, cs], n)
            carry_p[:, cs] = p[tm - n * SUB:]
            sa[:, cs] = cbw_ref[2:3, cs] * p + cbw_ref[1:2, cs] * p1 + cbw_ref[0:1, cs] * p2

    @pl.when(j == CBG)
    def _():
        for cs in col_blocks:
            bbf[:, cs] = ((proj(hn, cs) + bias_ref[CBG:CBG + 1, cs]) * sa[:, cs]).astype(bf16)

    @pl.when(j == PA)
    def _():
        for cs in col_blocks:
            sa[:, cs] = proj(abf, cs)

    @pl.when(j == GA)
    def _():
        for cs in col_blocks:
            sb[:, cs] = _sigmoid(proj(hn, cs) + bias_ref[GA - 1:GA, cs]) * sa[:, cs]

    @pl.when(j == PB)
    def _():
        for cs in col_blocks:
            sa[:, cs] = proj(bbf, cs)

    @pl.when(j == GB)
    def _():
        for cs in col_blocks:
            gate = _sigmoid(proj(hn, cs) + bias_ref[GB - 2:GB - 1, cs])
            abf[:, cs] = (sb[:, cs] + gate * sa[:, cs]).astype(bf16)

    @pl.when(j == WO)
    def _():
        for cs in col_blocks:
            o_ref[:, cs] = x_ref[:, cs] + proj(abf, cs)


def _mlp_kernel(x_ref, g2_ref, w_ref, gf_ref, o_ref, hn, ubf, *, final_norm):
    j = pl.program_id(1)
    tm = x_ref.shape[0]
    f32 = jnp.float32
    col_blocks = [slice(c, c + CB) for c in range(0, D, CB)]

    @pl.when(j == 0)
    def _():
        _rmsnorm_rows(x_ref, g2_ref, hn, tm)
        o_ref[...] = x_ref[...]

    @pl.when(j % 2 == 0)
    def _():
        for cs in col_blocks:
            u = jnp.maximum(jnp.dot(hn[...], w_ref[:, cs], preferred_element_type=f32), 0.0)
            ubf[:, cs] = (u * u).astype(ubf.dtype)

    @pl.when(j % 2 == 1)
    def _():
        for cs in col_blocks:
            o_ref[:, cs] += jnp.dot(ubf[...], w_ref[:, cs], preferred_element_type=f32)

    if final_norm:
        @pl.when(j == N_MLP - 1)
        def _():
            _rmsnorm_rows(o_ref, gf_ref, o_ref, tm)


def _row(v):
    return v.reshape(1, -1).astype(jnp.float32)


def _mixer(xf, g1, wmix, bmix, caw, cab, wr, wi, br, bi, lam, cbw, *, tiles_per_seq):
    t = xf.shape[0]
    full = lambda shape: pl.BlockSpec(shape, lambda i, j: (0,) * len(shape))
    return pl.pallas_call(
        functools.partial(_mixer_kernel, tiles_per_seq=tiles_per_seq),
        grid=(t // TM, N_MIX),
        in_specs=[
            pl.BlockSpec((TM, D), lambda i, j: (i, 0)),
            full((1, D)),
            pl.BlockSpec((None, D, D), lambda i, j: (j, 0, 0)),
            full(bmix.shape),
            full(caw.shape),
            full((1, D)),
            full(wr.shape),
            full(wi.shape),
            full((1, D)),
            full((1, D)),
            full((1, D)),
            full(cbw.shape),
        ],
        out_specs=pl.BlockSpec((TM, D), lambda i, j: (i, 0)),
        out_shape=jax.ShapeDtypeStruct((t, D), jnp.float32),
        scratch_shapes=[
            pltpu.VMEM((TM, D), jnp.bfloat16),
            pltpu.VMEM((TM, D), jnp.float32),
            pltpu.VMEM((TM, D), jnp.float32),
            pltpu.VMEM((TM, D), jnp.bfloat16),
            pltpu.VMEM((TM, D), jnp.bfloat16),
            pltpu.VMEM(((CONV_A_TAPS - 1) * SUB, D), jnp.float32),
            pltpu.VMEM(((CONV_B_TAPS - 1) * SUB, D), jnp.float32),
            pltpu.VMEM((SUB, D), jnp.float32),
        ],
        compiler_params=pltpu.CompilerParams(
            dimension_semantics=("arbitrary", "arbitrary"), vmem_limit_bytes=VMEM_LIMIT),
        name="mixer",
    )(xf, g1, wmix, bmix, caw, cab, wr, wi, br, bi, lam, cbw)


def _mlp(xf, g2, wmlp, gf, *, final_norm):
    t = xf.shape[0]
    return pl.pallas_call(
        functools.partial(_mlp_kernel, final_norm=final_norm),
        grid=(t // TM, N_MLP),
        in_specs=[
            pl.BlockSpec((TM, D), lambda i, j: (i, 0)),
            pl.BlockSpec((1, D), lambda i, j: (0, 0)),
            pl.BlockSpec((None, D, D), lambda i, j: (j, 0, 0)),
            pl.BlockSpec((1, D), lambda i, j: (0, 0)),
        ],
        out_specs=pl.BlockSpec((TM, D), lambda i, j: (i, 0)),
        out_shape=jax.ShapeDtypeStruct((t, D), jnp.float32),
        scratch_shapes=[
            pltpu.VMEM((TM, D), jnp.bfloat16),
            pltpu.VMEM((TM, D), jnp.bfloat16),
        ],
        compiler_params=pltpu.CompilerParams(
            dimension_semantics=("arbitrary", "arbitrary"), vmem_limit_bytes=VMEM_LIMIT),
        name="mlp",
    )(xf, g2, wmlp, gf)


def kernel(x, norm1_g, w_in, b_in, conv_a_w, conv_a_b, lru_wr, lru_br, lru_wi, lru_bi, lru_lam, conv_b_w, w_pa, w_pb, w_o, norm2_g, w_mlp1, w_mlp2, final_g):
    bsz, seq, d = x.shape
    depth = w_in.shape[0]
    assert d == D and seq % TM == 0
    bf16 = jnp.bfloat16
    n_tiles = bsz * seq // TM
    xf = x.reshape(n_tiles, SUB, SEG, d).transpose(0, 2, 1, 3).reshape(bsz * seq, d)
    n_ff = w_mlp1.shape[-1] // D
    assert 2 * n_ff == N_MLP
    for l in range(depth):
        seg = lambda k: w_in[l][:, k * D:(k + 1) * D]
        bseg = lambda k: b_in[l][k * D:(k + 1) * D]
        wmix = jnp.stack([seg(0), seg(1), seg(3), seg(4), seg(2), w_pa[l], seg(5), w_pb[l],
                          seg(6), w_o[l]]).astype(bf16)
        bmix = jnp.stack([bseg(0), bseg(1), bseg(3), bseg(4), bseg(2), bseg(5), bseg(6),
                          jnp.zeros((D,), jnp.float32)]).astype(jnp.float32)
        xf = _mixer(
            xf, _row(norm1_g[l]), wmix, bmix, conv_a_w[l], _row(conv_a_b[l]),
            lru_wr[l].astype(bf16), lru_wi[l].astype(bf16), _row(lru_br[l]), _row(lru_bi[l]),
            _row(lru_lam[l]), conv_b_w[l], tiles_per_seq=seq // TM)
        chunks = []
        for c in range(n_ff):
            chunks += [w_mlp1[l][:, c * D:(c + 1) * D], w_mlp2[l][c * D:(c + 1) * D, :]]
        wmlp = jnp.stack(chunks).astype(bf16)
        xf = _mlp(xf, _row(norm2_g[l]), wmlp, _row(final_g), final_norm=(l == depth - 1))
    return xf.reshape(n_tiles, SEG, SUB, d).transpose(0, 2, 1, 3).reshape(bsz, seq, d)
```

```python
import functools
import math

import jax
import jax.numpy as jnp
from jax import lax
from jax.experimental import pallas as pl
from jax.experimental.pallas import tpu as pltpu

D = 2048
LRU_BLOCK_W = 256
LRU_C = 8.0
EPS = 1e-6
TM = 1024
WN = 1024
CB = 512
ROWS = 64
SUB = 8
SEG = TM // SUB
CONV_A_TAPS = 4
CONV_B_TAPS = 3
VMEM_LIMIT = 60 * 1024 * 1024

N_PAIR = D // CB
N_WIDE = D // WN
T_XAYA = 0
T_CCX = T_XAYA + N_PAIR
T_CB = T_CCX + N_PAIR
T_PAGA = T_CB + N_WIDE
T_PBGB = T_PAGA + N_PAIR
T_WO = T_PBGB + N_PAIR
N_MIX = T_WO + N_WIDE
B_XA, B_YA, B_CC, B_CX, B_CB, B_GA, B_GB = range(7)


def _rmsnorm_rows(src_ref, g_ref, dst_ref, tm):
    def body(s, carry):
        r0 = pl.multiple_of(s * ROWS, ROWS)
        xs = src_ref[pl.ds(r0, ROWS), :]
        ms = jnp.mean(xs * xs, axis=-1, keepdims=True)
        y = xs * lax.rsqrt(ms + EPS) * g_ref[...]
        dst_ref[pl.ds(r0, ROWS), :] = y.astype(dst_ref.dtype)
        return carry

    lax.fori_loop(0, tm // ROWS, body, 0)


def _sigmoid(v):
    return 0.5 * jnp.tanh(0.5 * v) + 0.5


def _prev_tokens(z, tail_prev, n):
    tm = z.shape[0]
    row = lax.broadcasted_iota(jnp.int32, (SUB, z.shape[1]), 0)
    wrapped = []
    for v in range(n):
        lo = tm - (n - v) * SUB
        cur = z[lo:lo + SUB]
        prev = tail_prev[v * SUB:(v + 1) * SUB]
        wrapped.append(pltpu.roll(jnp.where(row == SUB - 1, prev, cur), 1, 0))
    return [jnp.concatenate(wrapped[n - k:] + [z[:tm - k * SUB]], axis=0) for k in range(1, n + 1)]


def _mixer_kernel(x_ref, g1_ref, w_ref, bias_ref, caw_ref, cab_ref, wr_ref, wi_ref,
                  br_ref, bi_ref, lam_ref, cbw_ref, o_ref,
                  hn, sa, abf, bbf, carry_xa, carry_p, carry_h, *, tiles_per_seq):
    i = pl.program_id(0)
    t = pl.program_id(1)
    tm = x_ref.shape[0]
    f32 = jnp.float32
    bf16 = jnp.bfloat16
    w_lo = slice(0, CB)
    w_hi = slice(CB, 2 * CB)

    def in_phase(first, n):
        return jnp.logical_and(t >= first, t < first + n)

    def dot(lhs_ref, wcols):
        return jnp.dot(lhs_ref[...], w_ref[:, wcols], preferred_element_type=f32)

    @pl.when(jnp.logical_and(t == 0, i % tiles_per_seq == 0))
    def _():
        carry_xa[...] = jnp.zeros_like(carry_xa)
        carry_p[...] = jnp.zeros_like(carry_p)
        carry_h[...] = jnp.zeros_like(carry_h)

    @pl.when(t == 0)
    def _():
        _rmsnorm_rows(x_ref, g1_ref, hn, tm)

    @pl.when(in_phase(T_XAYA, N_PAIR))
    def _():
        col = pl.multiple_of((t - T_XAYA) * CB, CB)
        cs = pl.ds(col, CB)
        z = dot(hn, w_lo) + bias_ref[B_XA:B_XA + 1, cs]
        ya = dot(hn, w_hi) + bias_ref[B_YA:B_YA + 1, cs]
        n = CONV_A_TAPS - 1
        z1, z2, z3 = _prev_tokens(z, carry_xa[:, cs], n)
        carry_xa[:, cs] = z[tm - n * SUB:]
        xc = (caw_ref[3:4, cs] * z + caw_ref[2:3, cs] * z1 + caw_ref[1:2, cs] * z2
              + caw_ref[0:1, cs] * z3 + cab_ref[:, cs])
        xcb = xc.astype(bf16)
        for q in range(CB // LRU_BLOCK_W):
            qs = slice(q * LRU_BLOCK_W, (q + 1) * LRU_BLOCK_W)
            gs = pl.ds(pl.multiple_of(col + q * LRU_BLOCK_W, LRU_BLOCK_W), LRU_BLOCK_W)
            blk_id = (t - T_XAYA) * (CB // LRU_BLOCK_W) + q
            r_pre = jnp.dot(xcb[:, qs], wr_ref[blk_id], preferred_element_type=f32) + br_ref[:, gs]
            i_pre = jnp.dot(xcb[:, qs], wi_ref[blk_id], preferred_element_type=f32) + bi_ref[:, gs]
            half_c = (-0.5 * LRU_C * math.log2(math.e)) * jax.nn.softplus(-lam_ref[:, gs])
            a = jnp.exp2(half_c * jnp.tanh(0.5 * r_pre) + half_c)
            v = (1.0 - a) * (1.0 + a)
            mult = jnp.where(v > 0.0, v * lax.rsqrt(v), 0.0)
            b = mult * (_sigmoid(i_pre) * xc[:, qs])
            h = b[0:SUB]
            p = a[0:SUB]
            hs, ps = [h], [p]
            for u in range(1, tm // SUB):
                au = a[u * SUB:(u + 1) * SUB]
                h = au * h + b[u * SUB:(u + 1) * SUB]
                p = au * p
                hs.append(h)
                ps.append(p)
            c = carry_h[0:1, gs]
            enter = []
            for s in range(SUB):
                enter.append(c)
                c = h[s:s + 1] + p[s:s + 1] * c
            carry_h[0:1, gs] = c
            enter = jnp.concatenate(enter, axis=0)
            h_all = jnp.concatenate([hu + pu * enter for hu, pu in zip(hs, ps)], axis=0)
            abf[:, gs] = (h_all * jax.nn.gelu(ya[:, qs])).astype(bf16)

    @pl.when(in_phase(T_CCX, N_PAIR))
    def _():
        cs = pl.ds(pl.multiple_of((t - T_CCX) * CB, CB), CB)
        p = (dot(hn, w_lo) + bias_ref[B_CC:B_CC + 1, cs]) * (dot(hn, w_hi) + bias_ref[B_CX:B_CX + 1, cs])
        n = CONV_B_TAPS - 1
        p1, p2 = _prev_tokens(p, carry_p[:, cs], n)
        carry_p[:, cs] = p[tm - n * SUB:]
        sa[:, cs] = cbw_ref[2:3, cs] * p + cbw_ref[1:2, cs] * p1 + cbw_ref[0:1, cs] * p2

    @pl.when(in_phase(T_CB, N_WIDE))
    def _():
        for k in range(WN // CB):
            cs = pl.ds(pl.multiple_of((t - T_CB) * WN + k * CB, CB), CB)
            cb = dot(hn, slice(k * CB, (k + 1) * CB)) + bias_ref[B_CB:B_CB + 1, cs]
            bbf[:, cs] = (cb * sa[:, cs]).astype(bf16)

    @pl.when(in_phase(T_PAGA, N_PAIR))
    def _():
        cs = pl.ds(pl.multiple_of((t - T_PAGA) * CB, CB), CB)
        sa[:, cs] = _sigmoid(dot(hn, w_hi) + bias_ref[B_GA:B_GA + 1, cs]) * dot(abf, w_lo)

    @pl.when(in_phase(T_PBGB, N_PAIR))
    def _():
        cs = pl.ds(pl.multiple_of((t - T_PBGB) * CB, CB), CB)
        gated_b = _sigmoid(dot(hn, w_hi) + bias_ref[B_GB:B_GB + 1, cs]) * dot(bbf, w_lo)
        abf[:, cs] = (sa[:, cs] + gated_b).astype(bf16)

    @pl.when(in_phase(T_WO, N_WIDE))
    def _():
        for k in range(WN // CB):
            ks = slice(k * CB, (k + 1) * CB)
            cs = pl.ds(pl.multiple_of((t - T_WO) * WN + k * CB, CB), CB)
            o_ref[:, ks] = x_ref[:, cs] + dot(abf, ks)


def _mlp_kernel(x_ref, g2_ref, w_ref, gf_ref, o_ref, hn, ubf, *, n_steps, final_norm):
    j = pl.program_id(1)
    tm = x_ref.shape[0]
    f32 = jnp.float32

    @pl.when(j == 0)
    def _():
        _rmsnorm_rows(x_ref, g2_ref, hn, tm)
        o_ref[...] = x_ref[...]

    @pl.when(j % 2 == 0)
    def _():
        for k in range(WN // CB):
            cs = slice(k * CB, (k + 1) * CB)
            u = jnp.maximum(jnp.dot(hn[...], w_ref[:, cs], preferred_element_type=f32), 0.0)
            ubf[:, cs] = (u * u).astype(ubf.dtype)

    @pl.when(j % 2 == 1)
    def _():
        for k in range(D // CB):
            rows = slice((k * CB // WN) * WN, (k * CB // WN + 1) * WN)
            lanes = slice(k * CB % WN, k * CB % WN + CB)
            o_ref[:, k * CB:(k + 1) * CB] += jnp.dot(
                ubf[...], w_ref[rows, lanes], preferred_element_type=f32)

    if final_norm:
        @pl.when(j == n_steps - 1)
        def _():
            _rmsnorm_rows(o_ref, gf_ref, o_ref, tm)


def _row(v):
    return v.reshape(1, -1).astype(jnp.float32)


def _mixer(xf, g1, wmix, bmix, caw, cab, wr, wi, br, bi, lam, cbw, *, tiles_per_seq):
    t = xf.shape[0]
    assert wmix.shape == (N_MIX, D, WN)
    full = lambda shape: pl.BlockSpec(shape, lambda i, j: (0,) * len(shape))
    return pl.pallas_call(
        functools.partial(_mixer_kernel, tiles_per_seq=tiles_per_seq),
        grid=(t // TM, N_MIX),
        in_specs=[
            pl.BlockSpec((TM, D), lambda i, j: (i, 0), pipeline_mode=pl.Buffered(1)),
            full((1, D)),
            pl.BlockSpec((None, D, WN), lambda i, j: (j, 0, 0)),
            full(bmix.shape),
            full(caw.shape),
            full((1, D)),
            full(wr.shape),
            full(wi.shape),
            full((1, D)),
            full((1, D)),
            full((1, D)),
            full(cbw.shape),
        ],
        out_specs=pl.BlockSpec((TM, WN), lambda i, j: (i, jnp.maximum(j - T_WO, 0))),
        out_shape=jax.ShapeDtypeStruct((t, D), jnp.float32),
        scratch_shapes=[
            pltpu.VMEM((TM, D), jnp.bfloat16),
            pltpu.VMEM((TM, D), jnp.float32),
            pltpu.VMEM((TM, D), jnp.bfloat16),
            pltpu.VMEM((TM, D), jnp.bfloat16),
            pltpu.VMEM(((CONV_A_TAPS - 1) * SUB, D), jnp.float32),
            pltpu.VMEM(((CONV_B_TAPS - 1) * SUB, D), jnp.float32),
            pltpu.VMEM((SUB, D), jnp.float32),
        ],
        compiler_params=pltpu.CompilerParams(
            dimension_semantics=("arbitrary", "arbitrary"), vmem_limit_bytes=VMEM_LIMIT),
        name="mixer",
    )(xf, g1, wmix, bmix, caw, cab, wr, wi, br, bi, lam, cbw)


def _mlp(xf, g2, wmlp, gf, *, final_norm):
    t = xf.shape[0]
    n_steps = wmlp.shape[0]
    return pl.pallas_call(
        functools.partial(_mlp_kernel, n_steps=n_steps, final_norm=final_norm),
        grid=(t // TM, n_steps),
        in_specs=[
            pl.BlockSpec((TM, D), lambda i, j: (i, 0)),
            pl.BlockSpec((1, D), lambda i, j: (0, 0)),
            pl.BlockSpec((None, D, WN), lambda i, j: (j, 0, 0)),
            pl.BlockSpec((1, D), lambda i, j: (0, 0)),
        ],
        out_specs=pl.BlockSpec((TM, D), lambda i, j: (i, 0)),
        out_shape=jax.ShapeDtypeStruct((t, D), jnp.float32),
        scratch_shapes=[
            pltpu.VMEM((TM, D), jnp.bfloat16),
            pltpu.VMEM((TM, WN), jnp.bfloat16),
        ],
        compiler_params=pltpu.CompilerParams(
            dimension_semantics=("arbitrary", "arbitrary"), vmem_limit_bytes=VMEM_LIMIT),
        name="mlp",
    )(xf, g2, wmlp, gf)


def _col_chunks(w, width):
    return [w[:, c:c + width] for c in range(0, w.shape[1], width)]


def _paired(w_first, w_second):
    return [jnp.concatenate(pair, axis=1)
            for pair in zip(_col_chunks(w_first, CB), _col_chunks(w_second, CB))]


def kernel(x, norm1_g, w_in, b_in, conv_a_w, conv_a_b, lru_wr, lru_br, lru_wi, lru_bi, lru_lam, conv_b_w, w_pa, w_pb, w_o, norm2_g, w_mlp1, w_mlp2, final_g):
    bsz, seq, d = x.shape
    depth = w_in.shape[0]
    assert d == D and seq % TM == 0
    bf16 = jnp.bfloat16
    n_tiles = bsz * seq // TM
    xf = x.reshape(n_tiles, SUB, SEG, d).transpose(0, 2, 1, 3).reshape(bsz * seq, d)
    d_ff = w_mlp1.shape[-1]
    for l in range(depth):
        xa, ya, cb, cc, cx, ga, gb = (w_in[l][:, k * D:(k + 1) * D] for k in range(7))
        b_xa, b_ya, b_cb, b_cc, b_cx, b_ga, b_gb = (b_in[l][k * D:(k + 1) * D] for k in range(7))
        wmix = jnp.stack(_paired(xa, ya) + _paired(cc, cx) + _col_chunks(cb, WN)
                         + _paired(w_pa[l], ga) + _paired(w_pb[l], gb) + _col_chunks(w_o[l], WN)).astype(bf16)
        bmix = jnp.stack([b_xa, b_ya, b_cc, b_cx, b_cb, b_ga, b_gb,
                          jnp.zeros((D,), jnp.float32)]).astype(jnp.float32)
        xf = _mixer(
            xf, _row(norm1_g[l]), wmix, bmix, conv_a_w[l], _row(conv_a_b[l]),
            lru_wr[l].astype(bf16), lru_wi[l].astype(bf16), _row(lru_br[l]), _row(lru_bi[l]),
            _row(lru_lam[l]), conv_b_w[l], tiles_per_seq=seq // TM)
        w1c = w_mlp1[l].reshape(D, d_ff // WN, WN).transpose(1, 0, 2)
        w2c = w_mlp2[l].reshape(d_ff // WN, WN, D // WN, WN).transpose(0, 2, 1, 3).reshape(d_ff // WN, D, WN)
        wmlp = jnp.stack([w1c, w2c], axis=1).reshape(2 * (d_ff // WN), D, WN).astype(bf16)
        xf = _mlp(xf, _row(norm2_g[l]), wmlp, _row(final_g), final_norm=(l == depth - 1))
    return xf.reshape(n_tiles, SEG, SUB, d).transpose(0, 2, 1, 3).reshape(bsz, seq, d)
```

```python
import functools
import math

import jax
import jax.numpy as jnp
from jax import lax
from jax.experimental import pallas as pl
from jax.experimental.pallas import tpu as pltpu

D = 2048
LRU_BLOCK_W = 256
LRU_C = 8.0
EPS = 1e-6
TM = 1024
WN = 1024
CB = 512
ROWS = 64
SUB = 8
SEG = TM // SUB
CONV_A_TAPS = 4
CONV_B_TAPS = 3
VMEM_LIMIT = 60 * 1024 * 1024

S_XA, S_YA, S_CB, S_CC, S_CX, S_GA, S_GB = range(7)
R_PA, R_PB, R_WO = range(3)
NB = D // CB
T_XAYA, T_CCX, T_CB, T_PAGA, T_PBGB, T_WO = (p * NB for p in range(6))
N_MIX = 6 * NB


def _phase_select(t, per_phase):
    out = per_phase[-1]
    for p in range(len(per_phase) - 2, -1, -1):
        out = jnp.where(t < (p + 1) * NB, per_phase[p], out)
    return out


def _lo_block(t):
    m = t % NB
    last = S_CB * NB + NB - 1
    return _phase_select(t, [S_XA * NB + m, S_CC * NB + m, S_CB * NB + m, last, last, last])


def _hi_block(t):
    m = t % NB
    return _phase_select(t, [S_YA * NB + m, S_CX * NB + m, S_CX * NB + NB - 1,
                             S_GA * NB + m, S_GB * NB + m, S_GB * NB + NB - 1])


def _proj_block(t):
    m = t % NB
    row = _phase_select(t, [R_PA, R_PA, R_PA, R_PA, R_PB, R_WO])
    col = _phase_select(t, [0, 0, 0, m, m, m])
    return row, col


def _rmsnorm_rows(src_ref, g_ref, dst_ref, tm):
    def body(s, carry):
        r0 = pl.multiple_of(s * ROWS, ROWS)
        xs = src_ref[pl.ds(r0, ROWS), :]
        ms = jnp.mean(xs * xs, axis=-1, keepdims=True)
        y = xs * lax.rsqrt(ms + EPS) * g_ref[...]
        dst_ref[pl.ds(r0, ROWS), :] = y.astype(dst_ref.dtype)
        return carry

    lax.fori_loop(0, tm // ROWS, body, 0)


def _sigmoid(v):
    return 0.5 * jnp.tanh(0.5 * v) + 0.5


def _prev_tokens(z, tail_prev, n):
    tm = z.shape[0]
    row = lax.broadcasted_iota(jnp.int32, (SUB, z.shape[1]), 0)
    wrapped = []
    for v in range(n):
        lo = tm - (n - v) * SUB
        cur = z[lo:lo + SUB]
        prev = tail_prev[v * SUB:(v + 1) * SUB]
        wrapped.append(pltpu.roll(jnp.where(row == SUB - 1, prev, cur), 1, 0))
    return [jnp.concatenate(wrapped[n - k:] + [z[:tm - k * SUB]], axis=0) for k in range(1, n + 1)]


def _mixer_kernel(x_ref, g1_ref, wlo_ref, whi_ref, wp_ref, bias_ref, caw_ref, cab_ref, wr_ref, wi_ref,
                  br_ref, bi_ref, lam_ref, cbw_ref, o_ref,
                  hn, sa, abf, bbf, carry_xa, carry_p, carry_h, *, tiles_per_seq):
    i = pl.program_id(0)
    t = pl.program_id(1)
    tm = x_ref.shape[0]
    f32 = jnp.float32
    bf16 = jnp.bfloat16
    col = pl.multiple_of((t % NB) * CB, CB)
    cs = pl.ds(col, CB)

    def in_phase(first):
        return jnp.logical_and(t >= first, t < first + NB)

    def dot(lhs_ref, w_ref):
        return jnp.dot(lhs_ref[...], w_ref[...], preferred_element_type=f32)

    def bias(seg):
        return bias_ref[seg:seg + 1, cs]

    @pl.when(jnp.logical_and(t == 0, i % tiles_per_seq == 0))
    def _():
        carry_xa[...] = jnp.zeros_like(carry_xa)
        carry_p[...] = jnp.zeros_like(carry_p)
        carry_h[...] = jnp.zeros_like(carry_h)

    @pl.when(t == 0)
    def _():
        _rmsnorm_rows(x_ref, g1_ref, hn, tm)

    @pl.when(in_phase(T_XAYA))
    def _():
        z = dot(hn, wlo_ref) + bias(S_XA)
        ya = dot(hn, whi_ref) + bias(S_YA)
        n = CONV_A_TAPS - 1
        z1, z2, z3 = _prev_tokens(z, carry_xa[:, cs], n)
        carry_xa[:, cs] = z[tm - n * SUB:]
        xc = (caw_ref[3:4, cs] * z + caw_ref[2:3, cs] * z1 + caw_ref[1:2, cs] * z2
              + caw_ref[0:1, cs] * z3 + cab_ref[:, cs])
        xcb = xc.astype(bf16)
        for q in range(CB // LRU_BLOCK_W):
            qs = slice(q * LRU_BLOCK_W, (q + 1) * LRU_BLOCK_W)
            gs = pl.ds(pl.multiple_of(col + q * LRU_BLOCK_W, LRU_BLOCK_W), LRU_BLOCK_W)
            blk_id = (t % NB) * (CB // LRU_BLOCK_W) + q
            r_pre = jnp.dot(xcb[:, qs], wr_ref[blk_id], preferred_element_type=f32) + br_ref[:, gs]
            i_pre = jnp.dot(xcb[:, qs], wi_ref[blk_id], preferred_element_type=f32) + bi_ref[:, gs]
            half_c = (-0.5 * LRU_C * math.log2(math.e)) * jax.nn.softplus(-lam_ref[:, gs])
            a = jnp.exp2(half_c * jnp.tanh(0.5 * r_pre) + half_c)
            v = (1.0 - a) * (1.0 + a)
            mult = jnp.where(v > 0.0, v * lax.rsqrt(v), 0.0)
            b = mult * (_sigmoid(i_pre) * xc[:, qs])
            h = b[0:SUB]
            p = a[0:SUB]
            hs, ps = [h], [p]
            for u in range(1, tm // SUB):
                au = a[u * SUB:(u + 1) * SUB]
                h = au * h + b[u * SUB:(u + 1) * SUB]
                p = au * p
                hs.append(h)
                ps.append(p)
            c = carry_h[0:1, gs]
            enter = []
            for s in range(SUB):
                enter.append(c)
                c = h[s:s + 1] + p[s:s + 1] * c
            carry_h[0:1, gs] = c
            enter = jnp.concatenate(enter, axis=0)
            h_all = jnp.concatenate([hu + pu * enter for hu, pu in zip(hs, ps)], axis=0)
            abf[:, gs] = (h_all * jax.nn.gelu(ya[:, qs])).astype(bf16)

    @pl.when(in_phase(T_CCX))
    def _():
        p = (dot(hn, wlo_ref) + bias(S_CC)) * (dot(hn, whi_ref) + bias(S_CX))
        n = CONV_B_TAPS - 1
        p1, p2 = _prev_tokens(p, carry_p[:, cs], n)
        carry_p[:, cs] = p[tm - n * SUB:]
        sa[:, cs] = cbw_ref[2:3, cs] * p + cbw_ref[1:2, cs] * p1 + cbw_ref[0:1, cs] * p2

    @pl.when(in_phase(T_CB))
    def _():
        bbf[:, cs] = ((dot(hn, wlo_ref) + bias(S_CB)) * sa[:, cs]).astype(bf16)

    @pl.when(in_phase(T_PAGA))
    def _():
        sa[:, cs] = _sigmoid(dot(hn, whi_ref) + bias(S_GA)) * dot(abf, wp_ref)

    @pl.when(in_phase(T_PBGB))
    def _():
        gated_b = _sigmoid(dot(hn, whi_ref) + bias(S_GB)) * dot(bbf, wp_ref)
        abf[:, cs] = (sa[:, cs] + gated_b).astype(bf16)

    @pl.when(in_phase(T_WO))
    def _():
        o_ref[...] = x_ref[:, cs] + dot(abf, wp_ref)


def _mlp_kernel(x_ref, g2_ref, w1_ref, w2_ref, gf_ref, o_ref, hn, ubf, *, n_steps, final_norm):
    j = pl.program_id(1)
    tm = x_ref.shape[0]
    f32 = jnp.float32

    @pl.when(j == 0)
    def _():
        _rmsnorm_rows(x_ref, g2_ref, hn, tm)
        o_ref[...] = x_ref[...]

    @pl.when(j % 2 == 0)
    def _():
        for k in range(WN // CB):
            cs = slice(k * CB, (k + 1) * CB)
            u = jnp.maximum(jnp.dot(hn[...], w1_ref[:, cs], preferred_element_type=f32), 0.0)
            ubf[:, cs] = (u * u).astype(ubf.dtype)

    @pl.when(j % 2 == 1)
    def _():
        for k in range(D // CB):
            cs = slice(k * CB, (k + 1) * CB)
            o_ref[:, cs] += jnp.dot(ubf[...], w2_ref[:, cs], preferred_element_type=f32)

    if final_norm:
        @pl.when(j == n_steps - 1)
        def _():
            _rmsnorm_rows(o_ref, gf_ref, o_ref, tm)


def _mixer(l, xf, g1, w_in, w_proj, b_in, caw, cab, wr, wi, br, bi, lam, cbw, *, tiles_per_seq):
    t = xf.shape[0]
    layer = lambda shape: pl.BlockSpec((None,) + shape, lambda i, j: (l,) + (0,) * len(shape))
    return pl.pallas_call(
        functools.partial(_mixer_kernel, tiles_per_seq=tiles_per_seq),
        grid=(t // TM, N_MIX),
        in_specs=[
            pl.BlockSpec((TM, D), lambda i, j: (i, 0), pipeline_mode=pl.Buffered(1)),
            layer((1, D)),
            pl.BlockSpec((None, D, CB), lambda i, j: (l, 0, _lo_block(j))),
            pl.BlockSpec((None, D, CB), lambda i, j: (l, 0, _hi_block(j))),
            pl.BlockSpec((None, D, CB), lambda i, j: (l,) + _proj_block(j)),
            layer(b_in.shape[1:]),
            layer(caw.shape[1:]),
            layer((1, D)),
            layer(wr.shape[1:]),
            layer(wi.shape[1:]),
            layer((1, D)),
            layer((1, D)),
            layer((1, D)),
            layer(cbw.shape[1:]),
        ],
        out_specs=pl.BlockSpec((TM, CB), lambda i, j: (i, jnp.maximum(j - T_WO, 0))),
        out_shape=jax.ShapeDtypeStruct((t, D), jnp.float32),
        scratch_shapes=[
            pltpu.VMEM((TM, D), jnp.bfloat16),
            pltpu.VMEM((TM, D), jnp.float32),
            pltpu.VMEM((TM, D), jnp.bfloat16),
            pltpu.VMEM((TM, D), jnp.bfloat16),
            pltpu.VMEM(((CONV_A_TAPS - 1) * SUB, D), jnp.float32),
            pltpu.VMEM(((CONV_B_TAPS - 1) * SUB, D), jnp.float32),
            pltpu.VMEM((SUB, D), jnp.float32),
        ],
        compiler_params=pltpu.CompilerParams(
            dimension_semantics=("arbitrary", "arbitrary"), vmem_limit_bytes=VMEM_LIMIT),
        name="mixer",
    )(xf, g1, w_in, w_in, w_proj, b_in, caw, cab, wr, wi, br, bi, lam, cbw)


def _mlp(l, xf, g2, w1, w2, gf, *, final_norm):
    t = xf.shape[0]
    n_steps = 2 * (w1.shape[-1] // WN)
    return pl.pallas_call(
        functools.partial(_mlp_kernel, n_steps=n_steps, final_norm=final_norm),
        grid=(t // TM, n_steps),
        in_specs=[
            pl.BlockSpec((TM, D), lambda i, j: (i, 0), pipeline_mode=pl.Buffered(1)),
            pl.BlockSpec((None, 1, D), lambda i, j: (l, 0, 0)),
            pl.BlockSpec((None, D, WN), lambda i, j: (l, 0, j // 2)),
            pl.BlockSpec((None, WN, D), lambda i, j: (l, j // 2, 0)),
            pl.BlockSpec((1, D), lambda i, j: (0, 0)),
        ],
        out_specs=pl.BlockSpec((TM, D), lambda i, j: (i, 0)),
        out_shape=jax.ShapeDtypeStruct((t, D), jnp.float32),
        scratch_shapes=[
            pltpu.VMEM((TM, D), jnp.bfloat16),
            pltpu.VMEM((TM, WN), jnp.bfloat16),
        ],
        compiler_params=pltpu.CompilerParams(
            dimension_semantics=("arbitrary", "arbitrary"), vmem_limit_bytes=VMEM_LIMIT),
        name="mlp",
    )(xf, g2, w1, w2, gf)


def kernel(x, norm1_g, w_in, b_in, conv_a_w, conv_a_b, lru_wr, lru_br, lru_wi, lru_bi, lru_lam, conv_b_w, w_pa, w_pb, w_o, norm2_g, w_mlp1, w_mlp2, final_g):
    bsz, seq, d = x.shape
    depth = w_in.shape[0]
    assert d == D and seq % TM == 0 and w_in.shape[-1] == 7 * D
    bf16 = jnp.bfloat16
    n_tiles = bsz * seq // TM
    xf = x.reshape(n_tiles, SUB, SEG, d).transpose(0, 2, 1, 3).reshape(bsz * seq, d)
    w_in_b = w_in.astype(bf16)
    w_proj = jnp.concatenate([w_pa, w_pb, w_o], axis=1).astype(bf16)
    w1_b = w_mlp1.astype(bf16)
    w2_b = w_mlp2.astype(bf16)
    wr_b = lru_wr.astype(bf16)
    wi_b = lru_wi.astype(bf16)
    rows = lambda v: v.reshape(depth, 1, D)
    b_in3 = b_in.reshape(depth, 7, D)
    for l in range(depth):
        xf = _mixer(l, xf, rows(norm1_g), w_in_b, w_proj, b_in3, conv_a_w, rows(conv_a_b), wr_b, wi_b,
                    rows(lru_br.reshape(depth, D)), rows(lru_bi.reshape(depth, D)), rows(lru_lam),
                    conv_b_w, tiles_per_seq=seq // TM)
        xf = _mlp(l, xf, rows(norm2_g), w1_b, w2_b, final_g.reshape(1, D), final_norm=(l == depth - 1))
    return xf.reshape(n_tiles, SEG, SUB, d).transpose(0, 2, 1, 3).reshape(bsz, seq, d)
```

```python
import functools
import math

import jax
import jax.numpy as jnp
from jax import lax
from jax.experimental import pallas as pl
from jax.experimental.pallas import tpu as pltpu

D = 2048
LRU_BLOCK_W = 256
LRU_C = 8.0
EPS = 1e-6
TM = 1024
WN = 1024
CB = 512
ROWS = 64
SUB = 8
SEG = TM // SUB
CONV_A_TAPS = 4
CONV_B_TAPS = 3
VMEM_LIMIT = 60 * 1024 * 1024

S_XA, S_YA, S_CB, S_CC, S_CX, S_GA, S_GB = range(7)
R_PA, R_PB, R_WO = range(3)
NB = D // CB
T_XAYA, T_CCX, T_CB, T_PAGA, T_PBGB, T_WO = (p * NB for p in range(6))
N_MIX = 6 * NB


def _phase_select(t, per_phase):
    out = per_phase[-1]
    for p in range(len(per_phase) - 2, -1, -1):
        out = jnp.where(t < (p + 1) * NB, per_phase[p], out)
    return out


def _lo_block(t):
    m = t % NB
    last = S_CB * NB + NB - 1
    return _phase_select(t, [S_XA * NB + m, S_CC * NB + m, S_CB * NB + m, last, last, last])


def _hi_block(t):
    m = t % NB
    return _phase_select(t, [S_YA * NB + m, S_CX * NB + m, S_CX * NB + NB - 1,
                             S_GA * NB + m, S_GB * NB + m, S_GB * NB + NB - 1])


def _proj_block(t):
    m = t % NB
    row = _phase_select(t, [R_PA, R_PA, R_PA, R_PA, R_PB, R_WO])
    col = _phase_select(t, [0, 0, 0, m, m, m])
    return row, col


def _rmsnorm_rows(src_ref, g_ref, dst_ref, tm):
    def body(s, carry):
        r0 = pl.multiple_of(s * ROWS, ROWS)
        xs = src_ref[pl.ds(r0, ROWS), :]
        ms = jnp.mean(xs * xs, axis=-1, keepdims=True)
        y = xs * lax.rsqrt(ms + EPS) * g_ref[...]
        dst_ref[pl.ds(r0, ROWS), :] = y.astype(dst_ref.dtype)
        return carry

    lax.fori_loop(0, tm // ROWS, body, 0, unroll=2)


def _sigmoid(v):
    return 0.5 * jnp.tanh(0.5 * v) + 0.5


def _prev_tokens(z, tail_prev, n):
    tm = z.shape[0]
    row = lax.broadcasted_iota(jnp.int32, (SUB, z.shape[1]), 0)
    wrapped = []
    for v in range(n):
        lo = tm - (n - v) * SUB
        cur = z[lo:lo + SUB]
        prev = tail_prev[v * SUB:(v + 1) * SUB]
        wrapped.append(pltpu.roll(jnp.where(row == SUB - 1, prev, cur), 1, 0))
    return [jnp.concatenate(wrapped[n - k:] + [z[:tm - k * SUB]], axis=0) for k in range(1, n + 1)]


def _mixer_kernel(x_ref, g1_ref, wlo_ref, whi_ref, wp_ref, bias_ref, caw_ref, cab_ref, wr_ref, wi_ref,
                  br_ref, bi_ref, lam_ref, cbw_ref, o_ref,
                  hn, sa, abf, bbf, carry_xa, carry_p, carry_h, *, tiles_per_seq):
    i = pl.program_id(0)
    t = pl.program_id(1)
    tm = x_ref.shape[0]
    f32 = jnp.float32
    bf16 = jnp.bfloat16
    col = pl.multiple_of((t % NB) * CB, CB)
    cs = pl.ds(col, CB)

    def in_phase(first):
        return jnp.logical_and(t >= first, t < first + NB)

    def dot(lhs_ref, w_ref):
        return jnp.dot(lhs_ref[...], w_ref[...], preferred_element_type=f32)

    def bias(seg):
        return bias_ref[seg:seg + 1, cs]

    @pl.when(jnp.logical_and(t == 0, i % tiles_per_seq == 0))
    def _():
        carry_xa[...] = jnp.zeros_like(carry_xa)
        carry_p[...] = jnp.zeros_like(carry_p)
        carry_h[...] = jnp.zeros_like(carry_h)

    def gate_cols(q):
        return pl.ds(pl.multiple_of(col + q * LRU_BLOCK_W, LRU_BLOCK_W), LRU_BLOCK_W)

    def gates_a(z):
        n = CONV_A_TAPS - 1
        z1, z2, z3 = _prev_tokens(z, carry_xa[:, cs], n)
        carry_xa[:, cs] = z[tm - n * SUB:]
        xc = (caw_ref[3:4, cs] * z + caw_ref[2:3, cs] * z1 + caw_ref[1:2, cs] * z2
              + caw_ref[0:1, cs] * z3 + cab_ref[:, cs])
        xcb = xc.astype(bf16)
        pre = []
        for q in range(CB // LRU_BLOCK_W):
            qs = slice(q * LRU_BLOCK_W, (q + 1) * LRU_BLOCK_W)
            gs = gate_cols(q)
            blk_id = (t % NB) * (CB // LRU_BLOCK_W) + q
            pre.append((
                jnp.dot(xcb[:, qs], wr_ref[blk_id], preferred_element_type=f32) + br_ref[:, gs],
                jnp.dot(xcb[:, qs], wi_ref[blk_id], preferred_element_type=f32) + bi_ref[:, gs]))
        return xc, pre

    def recurrence_a(xc, pre, ya):
        for q, (r_pre, i_pre) in enumerate(pre):
            qs = slice(q * LRU_BLOCK_W, (q + 1) * LRU_BLOCK_W)
            gs = gate_cols(q)
            half_c = (-0.5 * LRU_C * math.log2(math.e)) * jax.nn.softplus(-lam_ref[:, gs])
            a = jnp.exp2(half_c * jnp.tanh(0.5 * r_pre) + half_c)
            v = (1.0 - a) * (1.0 + a)
            mult = jnp.where(v > 0.0, v * lax.rsqrt(v), 0.0)
            b = mult * (_sigmoid(i_pre) * xc[:, qs])
            h = b[0:SUB]
            p = a[0:SUB]
            hs, ps = [h], [p]
            for u in range(1, tm // SUB):
                au = a[u * SUB:(u + 1) * SUB]
                h = au * h + b[u * SUB:(u + 1) * SUB]
                p = au * p
                hs.append(h)
                ps.append(p)
            c = carry_h[0:1, gs]
            enter = []
            for s in range(SUB):
                enter.append(c)
                c = h[s:s + 1] + p[s:s + 1] * c
            carry_h[0:1, gs] = c
            enter = jnp.concatenate(enter, axis=0)
            h_all = jnp.concatenate([hu + pu * enter for hu, pu in zip(hs, ps)], axis=0)
            abf[:, gs] = (h_all * jax.nn.gelu(ya[:, qs])).astype(bf16)

    def conv_b():
        p = (dot(hn, wlo_ref) + bias(S_CC)) * (dot(hn, whi_ref) + bias(S_CX))
        n = CONV_B_TAPS - 1
        p1, p2 = _prev_tokens(p, carry_p[:, cs], n)
        carry_p[:, cs] = p[tm - n * SUB:]
        sa[:, cs] = cbw_ref[2:3, cs] * p + cbw_ref[1:2, cs] * p1 + cbw_ref[0:1, cs] * p2

    @pl.when(t == 0)
    def _():
        _rmsnorm_rows(x_ref, g1_ref, hn, tm)

    @pl.when(in_phase(T_XAYA))
    def _():
        z = dot(hn, wlo_ref) + bias(S_XA)
        ya = dot(hn, whi_ref) + bias(S_YA)
        xc, pre = gates_a(z)
        recurrence_a(xc, pre, ya)

    @pl.when(in_phase(T_CCX))
    def _():
        conv_b()

    @pl.when(in_phase(T_CB))
    def _():
        bbf[:, cs] = ((dot(hn, wlo_ref) + bias(S_CB)) * sa[:, cs]).astype(bf16)

    @pl.when(in_phase(T_PAGA))
    def _():
        sa[:, cs] = _sigmoid(dot(hn, whi_ref) + bias(S_GA)) * dot(abf, wp_ref)

    @pl.when(in_phase(T_PBGB))
    def _():
        gated_b = _sigmoid(dot(hn, whi_ref) + bias(S_GB)) * dot(bbf, wp_ref)
        abf[:, cs] = (sa[:, cs] + gated_b).astype(bf16)

    @pl.when(in_phase(T_WO))
    def _():
        o_ref[...] = x_ref[:, cs] + dot(abf, wp_ref)


def _mlp_kernel(x_ref, g2_ref, w1_ref, w2_ref, gf_ref, o_ref, hn, ubf, *, n_steps, final_norm):
    j = pl.program_id(1)
    tm = x_ref.shape[0]
    f32 = jnp.float32

    @pl.when(j == 0)
    def _():
        _rmsnorm_rows(x_ref, g2_ref, hn, tm)
        o_ref[...] = x_ref[...]

    @pl.when(j % 2 == 0)
    def _():
        for k in range(WN // CB):
            cs = slice(k * CB, (k + 1) * CB)
            u = jnp.maximum(jnp.dot(hn[...], w1_ref[:, cs], preferred_element_type=f32), 0.0)
            ubf[:, cs] = (u * u).astype(ubf.dtype)

    @pl.when(j % 2 == 1)
    def _():
        for k in range(D // CB):
            cs = slice(k * CB, (k + 1) * CB)
            o_ref[:, cs] += jnp.dot(ubf[...], w2_ref[:, cs], preferred_element_type=f32)

    if final_norm:
        @pl.when(j == n_steps - 1)
        def _():
            _rmsnorm_rows(o_ref, gf_ref, o_ref, tm)


def _mixer(l, xf, g1, w_in, w_proj, b_in, caw, cab, wr, wi, br, bi, lam, cbw, *, tiles_per_seq):
    t = xf.shape[0]
    layer = lambda shape: pl.BlockSpec((None,) + shape, lambda i, j: (l,) + (0,) * len(shape))
    return pl.pallas_call(
        functools.partial(_mixer_kernel, tiles_per_seq=tiles_per_seq),
        grid=(t // TM, N_MIX),
        in_specs=[
            pl.BlockSpec((TM, D), lambda i, j: (i, 0), pipeline_mode=pl.Buffered(1)),
            layer((1, D)),
            pl.BlockSpec((None, D, CB), lambda i, j: (l, 0, _lo_block(j))),
            pl.BlockSpec((None, D, CB), lambda i, j: (l, 0, _hi_block(j))),
            pl.BlockSpec((None, D, CB), lambda i, j: (l,) + _proj_block(j)),
            layer(b_in.shape[1:]),
            layer(caw.shape[1:]),
            layer((1, D)),
            layer(wr.shape[1:]),
            layer(wi.shape[1:]),
            layer((1, D)),
            layer((1, D)),
            layer((1, D)),
            layer(cbw.shape[1:]),
        ],
        out_specs=pl.BlockSpec((TM, CB), lambda i, j: (i, jnp.maximum(j - T_WO, 0))),
        out_shape=jax.ShapeDtypeStruct((t, D), jnp.float32),
        scratch_shapes=[
            pltpu.VMEM((TM, D), jnp.bfloat16),
            pltpu.VMEM((TM, D), jnp.float32),
            pltpu.VMEM((TM, D), jnp.bfloat16),
            pltpu.VMEM((TM, D), jnp.bfloat16),
            pltpu.VMEM(((CONV_A_TAPS - 1) * SUB, D), jnp.float32),
            pltpu.VMEM(((CONV_B_TAPS - 1) * SUB, D), jnp.float32),
            pltpu.VMEM((SUB, D), jnp.float32),
        ],
        compiler_params=pltpu.CompilerParams(
            dimension_semantics=("arbitrary", "arbitrary"), vmem_limit_bytes=VMEM_LIMIT),
        name="mixer",
    )(xf, g1, w_in, w_in, w_proj, b_in, caw, cab, wr, wi, br, bi, lam, cbw)


def _mlp(l, xf, g2, w1, w2, gf, *, final_norm):
    t = xf.shape[0]
    n_steps = 2 * (w1.shape[-1] // WN)
    return pl.pallas_call(
        functools.partial(_mlp_kernel, n_steps=n_steps, final_norm=final_norm),
        grid=(t // TM, n_steps),
        in_specs=[
            pl.BlockSpec((TM, D), lambda i, j: (i, 0)),
            pl.BlockSpec((None, 1, D), lambda i, j: (l, 0, 0)),
            pl.BlockSpec((None, D, WN), lambda i, j: (l, 0, j // 2)),
            pl.BlockSpec((None, WN, D), lambda i, j: (l, j // 2, 0)),
            pl.BlockSpec((1, D), lambda i, j: (0, 0)),
        ],
        out_specs=pl.BlockSpec((TM, D), lambda i, j: (i, 0)),
        out_shape=jax.ShapeDtypeStruct((t, D), jnp.float32),
        scratch_shapes=[
            pltpu.VMEM((TM, D), jnp.bfloat16),
            pltpu.VMEM((TM, WN), jnp.bfloat16),
        ],
        compiler_params=pltpu.CompilerParams(
            dimension_semantics=("arbitrary", "arbitrary"), vmem_limit_bytes=VMEM_LIMIT),
        name="mlp",
    )(xf, g2, w1, w2, gf)


def kernel(x, norm1_g, w_in, b_in, conv_a_w, conv_a_b, lru_wr, lru_br, lru_wi, lru_bi, lru_lam, conv_b_w, w_pa, w_pb, w_o, norm2_g, w_mlp1, w_mlp2, final_g):
    bsz, seq, d = x.shape
    depth = w_in.shape[0]
    assert d == D and seq % TM == 0 and w_in.shape[-1] == 7 * D
    bf16 = jnp.bfloat16
    n_tiles = bsz * seq // TM
    xf = x.reshape(n_tiles, SUB, SEG, d).transpose(0, 2, 1, 3).reshape(bsz * seq, d)
    w_in_b = w_in.astype(bf16)
    w_proj = jnp.concatenate([w_pa, w_pb, w_o], axis=1).astype(bf16)
    w1_b = w_mlp1.astype(bf16)
    w2_b = w_mlp2.astype(bf16)
    wr_b = lru_wr.astype(bf16)
    wi_b = lru_wi.astype(bf16)
    rows = lambda v: v.reshape(depth, 1, D)
    b_in3 = b_in.reshape(depth, 7, D)
    for l in range(depth):
        xf = _mixer(l, xf, rows(norm1_g), w_in_b, w_proj, b_in3, conv_a_w, rows(conv_a_b), wr_b, wi_b,
                    rows(lru_br.reshape(depth, D)), rows(lru_bi.reshape(depth, D)), rows(lru_lam),
                    conv_b_w, tiles_per_seq=seq // TM)
        xf = _mlp(l, xf, rows(norm2_g), w1_b, w2_b, final_g.reshape(1, D), final_norm=(l == depth - 1))
    return xf.reshape(n_tiles, SEG, SUB, d).transpose(0, 2, 1, 3).reshape(bsz, seq, d)
```

```python
import functools
import math

import jax
import jax.numpy as jnp
from jax import lax
from jax.experimental import pallas as pl
from jax.experimental.pallas import tpu as pltpu

D = 2048
LRU_BLOCK_W = 256
LRU_C = 8.0
EPS = 1e-6
TM = 1024
WN = 1024
CB = 512
ROWS = 64
SUB = 8
SEG = TM // SUB
CONV_A_TAPS = 4
CONV_B_TAPS = 3
VMEM_LIMIT = 60 * 1024 * 1024

S_XA, S_YA, S_CB, S_CC, S_CX, S_GA, S_GB = range(7)
R_PA, R_PB, R_WO = range(3)
NB = D // CB
T_XAYA, T_CCX, T_PAGA, T_PBGB, T_WO = (p * NB for p in range(5))
N_MIX = 5 * NB


def _phase_select(t, per_phase):
    out = per_phase[-1]
    for p in range(len(per_phase) - 2, -1, -1):
        out = jnp.where(t < (p + 1) * NB, per_phase[p], out)
    return out


def _lo_block(t):
    m = t % NB
    last = S_CB * NB + NB - 1
    return _phase_select(t, [S_XA * NB + m, S_CC * NB + m, S_CB * NB + m, last, last])


def _hi_block(t):
    m = t % NB
    return _phase_select(t, [S_YA * NB + m, S_CX * NB + m, S_GA * NB + m, S_GB * NB + m,
                             S_GB * NB + NB - 1])


def _proj_block(t):
    m = t % NB
    row = _phase_select(t, [R_PA, R_PA, R_PA, R_PB, R_WO])
    col = _phase_select(t, [0, 0, m, m, m])
    return row, col


def _rmsnorm_rows(src_ref, g_ref, dst_ref, tm):
    def body(s, carry):
        r0 = pl.multiple_of(s * ROWS, ROWS)
        xs = src_ref[pl.ds(r0, ROWS), :]
        ms = jnp.mean(xs * xs, axis=-1, keepdims=True)
        y = xs * lax.rsqrt(ms + EPS) * g_ref[...]
        dst_ref[pl.ds(r0, ROWS), :] = y.astype(dst_ref.dtype)
        return carry

    lax.fori_loop(0, tm // ROWS, body, 0, unroll=2)


def _sigmoid(v):
    return 0.5 * jnp.tanh(0.5 * v) + 0.5


def _prev_tokens(z, tail_prev, n):
    tm = z.shape[0]
    row = lax.broadcasted_iota(jnp.int32, (SUB, z.shape[1]), 0)
    wrapped = []
    for v in range(n):
        lo = tm - (n - v) * SUB
        cur = z[lo:lo + SUB]
        prev = tail_prev[v * SUB:(v + 1) * SUB]
        wrapped.append(pltpu.roll(jnp.where(row == SUB - 1, prev, cur), 1, 0))
    return [jnp.concatenate(wrapped[n - k:] + [z[:tm - k * SUB]], axis=0) for k in range(1, n + 1)]


def _mixer_kernel(x_ref, g1_ref, wlo_ref, whi_ref, wp_ref, bias_ref, caw_ref, cab_ref, wr_ref, wi_ref,
                  br_ref, bi_ref, lam_ref, cbw_ref, o_ref,
                  hn, sa, abf, bbf, carry_xa, carry_p, carry_h, *, tiles_per_seq):
    i = pl.program_id(0)
    t = pl.program_id(1)
    tm = x_ref.shape[0]
    f32 = jnp.float32
    bf16 = jnp.bfloat16
    col = pl.multiple_of((t % NB) * CB, CB)
    cs = pl.ds(col, CB)

    def in_phase(first):
        return jnp.logical_and(t >= first, t < first + NB)

    def dot(lhs_ref, w_ref):
        return jnp.dot(lhs_ref[...], w_ref[...], preferred_element_type=f32)

    def bias(seg):
        return bias_ref[seg:seg + 1, cs]

    @pl.when(jnp.logical_and(t == 0, i % tiles_per_seq == 0))
    def _():
        carry_xa[...] = jnp.zeros_like(carry_xa)
        carry_p[...] = jnp.zeros_like(carry_p)
        carry_h[...] = jnp.zeros_like(carry_h)

    def gate_cols(q):
        return pl.ds(pl.multiple_of(col + q * LRU_BLOCK_W, LRU_BLOCK_W), LRU_BLOCK_W)

    def gates_a(z):
        n = CONV_A_TAPS - 1
        z1, z2, z3 = _prev_tokens(z, carry_xa[:, cs], n)
        carry_xa[:, cs] = z[tm - n * SUB:]
        xc = (caw_ref[3:4, cs] * z + caw_ref[2:3, cs] * z1 + caw_ref[1:2, cs] * z2
              + caw_ref[0:1, cs] * z3 + cab_ref[:, cs])
        xcb = xc.astype(bf16)
        pre = []
        for q in range(CB // LRU_BLOCK_W):
            qs = slice(q * LRU_BLOCK_W, (q + 1) * LRU_BLOCK_W)
            gs = gate_cols(q)
            blk_id = (t % NB) * (CB // LRU_BLOCK_W) + q
            pre.append((
                jnp.dot(xcb[:, qs], wr_ref[blk_id], preferred_element_type=f32) + br_ref[:, gs],
                jnp.dot(xcb[:, qs], wi_ref[blk_id], preferred_element_type=f32) + bi_ref[:, gs]))
        return xc, pre

    def recurrence_a(xc, pre, ya):
        for q, (r_pre, i_pre) in enumerate(pre):
            qs = slice(q * LRU_BLOCK_W, (q + 1) * LRU_BLOCK_W)
            gs = gate_cols(q)
            half_c = (-0.5 * LRU_C * math.log2(math.e)) * jax.nn.softplus(-lam_ref[:, gs])
            a = jnp.exp2(half_c * jnp.tanh(0.5 * r_pre) + half_c)
            v = (1.0 - a) * (1.0 + a)
            mult = jnp.where(v > 0.0, v * lax.rsqrt(v), 0.0)
            b = mult * (_sigmoid(i_pre) * xc[:, qs])
            h = b[0:SUB]
            p = a[0:SUB]
            hs, ps = [h], [p]
            for u in range(1, tm // SUB):
                au = a[u * SUB:(u + 1) * SUB]
                h = au * h + b[u * SUB:(u + 1) * SUB]
                p = au * p
                hs.append(h)
                ps.append(p)
            c = carry_h[0:1, gs]
            enter = []
            for s in range(SUB):
                enter.append(c)
                c = h[s:s + 1] + p[s:s + 1] * c
            carry_h[0:1, gs] = c
            enter = jnp.concatenate(enter, axis=0)
            h_all = jnp.concatenate([hu + pu * enter for hu, pu in zip(hs, ps)], axis=0)
            abf[:, gs] = (h_all * jax.nn.gelu(ya[:, qs])).astype(bf16)

    def conv_b():
        p = (dot(hn, wlo_ref) + bias(S_CC)) * (dot(hn, whi_ref) + bias(S_CX))
        n = CONV_B_TAPS - 1
        p1, p2 = _prev_tokens(p, carry_p[:, cs], n)
        carry_p[:, cs] = p[tm - n * SUB:]
        sa[:, cs] = cbw_ref[2:3, cs] * p + cbw_ref[1:2, cs] * p1 + cbw_ref[0:1, cs] * p2

    @pl.when(t == 0)
    def _():
        _rmsnorm_rows(x_ref, g1_ref, hn, tm)

    @pl.when(in_phase(T_XAYA))
    def _():
        z = dot(hn, wlo_ref) + bias(S_XA)
        ya = dot(hn, whi_ref) + bias(S_YA)
        xc, pre = gates_a(z)
        recurrence_a(xc, pre, ya)

    @pl.when(in_phase(T_CCX))
    def _():
        conv_b()

    @pl.when(in_phase(T_PAGA))
    def _():
        bbf[:, cs] = ((dot(hn, wlo_ref) + bias(S_CB)) * sa[:, cs]).astype(bf16)
        sa[:, cs] = _sigmoid(dot(hn, whi_ref) + bias(S_GA)) * dot(abf, wp_ref)

    @pl.when(in_phase(T_PBGB))
    def _():
        gated_b = _sigmoid(dot(hn, whi_ref) + bias(S_GB)) * dot(bbf, wp_ref)
        abf[:, cs] = (sa[:, cs] + gated_b).astype(bf16)

    @pl.when(in_phase(T_WO))
    def _():
        o_ref[...] = x_ref[:, cs] + dot(abf, wp_ref)


def _mlp_kernel(x_ref, g2_ref, w1_ref, w2_ref, gf_ref, o_ref, hn, ubf, *, n_steps, final_norm):
    j = pl.program_id(1)
    tm = x_ref.shape[0]
    f32 = jnp.float32

    @pl.when(j == 0)
    def _():
        _rmsnorm_rows(x_ref, g2_ref, hn, tm)
        o_ref[...] = x_ref[...]

    for k in range(WN // CB):
        cs = slice(k * CB, (k + 1) * CB)
        u = jnp.maximum(jnp.dot(hn[...], w1_ref[:, cs], preferred_element_type=f32), 0.0)
        ubf[:, cs] = (u * u).astype(ubf.dtype)
    for k in range(D // CB):
        cs = slice(k * CB, (k + 1) * CB)
        o_ref[:, cs] += jnp.dot(ubf[...], w2_ref[:, cs], preferred_element_type=f32)

    if final_norm:
        @pl.when(j == n_steps - 1)
        def _():
            _rmsnorm_rows(o_ref, gf_ref, o_ref, tm)


def _mixer(l, xf, g1, w_in, w_proj, b_in, caw, cab, wr, wi, br, bi, lam, cbw, *, tiles_per_seq):
    t = xf.shape[0]
    layer = lambda shape: pl.BlockSpec((None,) + shape, lambda i, j: (l,) + (0,) * len(shape))
    return pl.pallas_call(
        functools.partial(_mixer_kernel, tiles_per_seq=tiles_per_seq),
        grid=(t // TM, N_MIX),
        in_specs=[
            pl.BlockSpec((TM, D), lambda i, j: (i, 0), pipeline_mode=pl.Buffered(1)),
            layer((1, D)),
            pl.BlockSpec((None, D, CB), lambda i, j: (l, 0, _lo_block(j))),
            pl.BlockSpec((None, D, CB), lambda i, j: (l, 0, _hi_block(j))),
            pl.BlockSpec((None, D, CB), lambda i, j: (l,) + _proj_block(j)),
            layer(b_in.shape[1:]),
            layer(caw.shape[1:]),
            layer((1, D)),
            layer(wr.shape[1:]),
            layer(wi.shape[1:]),
            layer((1, D)),
            layer((1, D)),
            layer((1, D)),
            layer(cbw.shape[1:]),
        ],
        out_specs=pl.BlockSpec((TM, CB), lambda i, j: (i, jnp.maximum(j - T_WO, 0))),
        out_shape=jax.ShapeDtypeStruct((t, D), jnp.float32),
        scratch_shapes=[
            pltpu.VMEM((TM, D), jnp.bfloat16),
            pltpu.VMEM((TM, D), jnp.float32),
            pltpu.VMEM((TM, D), jnp.bfloat16),
            pltpu.VMEM((TM, D), jnp.bfloat16),
            pltpu.VMEM(((CONV_A_TAPS - 1) * SUB, D), jnp.float32),
            pltpu.VMEM(((CONV_B_TAPS - 1) * SUB, D), jnp.float32),
            pltpu.VMEM((SUB, D), jnp.float32),
        ],
        compiler_params=pltpu.CompilerParams(
            dimension_semantics=("arbitrary", "arbitrary"), vmem_limit_bytes=VMEM_LIMIT),
        name="mixer",
    )(xf, g1, w_in, w_in, w_proj, b_in, caw, cab, wr, wi, br, bi, lam, cbw)


def _mlp(l, xf, g2, w1, w2, gf, *, final_norm):
    t = xf.shape[0]
    n_steps = w1.shape[-1] // WN
    return pl.pallas_call(
        functools.partial(_mlp_kernel, n_steps=n_steps, final_norm=final_norm),
        grid=(t // TM, n_steps),
        in_specs=[
            pl.BlockSpec((TM, D), lambda i, j: (i, 0)),
            pl.BlockSpec((None, 1, D), lambda i, j: (l, 0, 0)),
            pl.BlockSpec((None, D, WN), lambda i, j: (l, 0, j)),
            pl.BlockSpec((None, WN, D), lambda i, j: (l, j, 0)),
            pl.BlockSpec((1, D), lambda i, j: (0, 0)),
        ],
        out_specs=pl.BlockSpec((TM, D), lambda i, j: (i, 0)),
        out_shape=jax.ShapeDtypeStruct((t, D), jnp.float32),
        scratch_shapes=[
            pltpu.VMEM((TM, D), jnp.bfloat16),
            pltpu.VMEM((TM, WN), jnp.bfloat16),
        ],
        compiler_params=pltpu.CompilerParams(
            dimension_semantics=("arbitrary", "arbitrary"), vmem_limit_bytes=VMEM_LIMIT),
        name="mlp",
    )(xf, g2, w1, w2, gf)


def kernel(x, norm1_g, w_in, b_in, conv_a_w, conv_a_b, lru_wr, lru_br, lru_wi, lru_bi, lru_lam, conv_b_w, w_pa, w_pb, w_o, norm2_g, w_mlp1, w_mlp2, final_g):
    bsz, seq, d = x.shape
    depth = w_in.shape[0]
    assert d == D and seq % TM == 0 and w_in.shape[-1] == 7 * D
    bf16 = jnp.bfloat16
    n_tiles = bsz * seq // TM
    xf = x.reshape(n_tiles, SUB, SEG, d).transpose(0, 2, 1, 3).reshape(bsz * seq, d)
    w_in_b = w_in.astype(bf16)
    w_proj = jnp.concatenate([w_pa, w_pb, w_o], axis=1).astype(bf16)
    w1_b = w_mlp1.astype(bf16)
    w2_b = w_mlp2.astype(bf16)
    wr_b = lru_wr.astype(bf16)
    wi_b = lru_wi.astype(bf16)
    rows = lambda v: v.reshape(depth, 1, D)
    b_in3 = b_in.reshape(depth, 7, D)
    for l in range(depth):
        xf = _mixer(l, xf, rows(norm1_g), w_in_b, w_proj, b_in3, conv_a_w, rows(conv_a_b), wr_b, wi_b,
                    rows(lru_br.reshape(depth, D)), rows(lru_bi.reshape(depth, D)), rows(lru_lam),
                    conv_b_w, tiles_per_seq=seq // TM)
        xf = _mlp(l, xf, rows(norm2_g), w1_b, w2_b, final_g.reshape(1, D), final_norm=(l == depth - 1))
    return xf.reshape(n_tiles, SEG, SUB, d).transpose(0, 2, 1, 3).reshape(bsz, seq, d)
```

```python
import functools
import math

import jax
import jax.numpy as jnp
from jax import lax
from jax.experimental import pallas as pl
from jax.experimental.pallas import tpu as pltpu

D = 2048
LRU_BLOCK_W = 256
LRU_C = 8.0
EPS = 1e-6
TM = 1024
WN = 1024
CB = 512
ROWS = 64
SUB = 8
SEG = TM // SUB
CONV_A_TAPS = 4
CONV_B_TAPS = 3
VMEM_LIMIT = 60 * 1024 * 1024

S_XA, S_YA, S_CB, S_CC, S_CX, S_GA, S_GB = range(7)
R_PA, R_PB, R_WO = range(3)
NB = D // CB
T_XAYA, T_CCX, T_PAGA, T_PBGB, T_WO = (p * NB for p in range(5))
N_MIX = 5 * NB


def _phase_select(t, per_phase):
    out = per_phase[-1]
    for p in range(len(per_phase) - 2, -1, -1):
        out = jnp.where(t < (p + 1) * NB, per_phase[p], out)
    return out


def _lo_block(t):
    m = t % NB
    last = S_CB * NB + NB - 1
    return _phase_select(t, [S_XA * NB + m, S_CC * NB + m, S_CB * NB + m, last, last])


def _hi_block(t):
    m = t % NB
    return _phase_select(t, [S_YA * NB + m, S_CX * NB + m, S_GA * NB + m, S_GB * NB + m,
                             S_GB * NB + NB - 1])


def _proj_block(t):
    m = t % NB
    row = _phase_select(t, [R_PA, R_PA, R_PA, R_PB, R_WO])
    col = _phase_select(t, [0, 0, m, m, m])
    return row, col


def _interleaved_rows(src_ref, group, cols=slice(None)):
    seg = src_ref.shape[0] // SUB
    aligned = (lambda r: r) if isinstance(group, int) else (lambda r: pl.multiple_of(r, SUB))
    slabs = [src_ref[pl.ds(aligned(s * seg + group * SUB), SUB), cols] for s in range(SUB)]
    blk = jnp.swapaxes(jnp.stack(slabs, axis=0), 0, 1)
    return blk.reshape(SUB * SUB, blk.shape[-1])


def _rmsnorm_rows(src_ref, g_ref, dst_ref, tm, interleave=False):
    assert ROWS == SUB * SUB

    def body(s, carry):
        r0 = pl.multiple_of(s * ROWS, ROWS)
        xs = _interleaved_rows(src_ref, s) if interleave else src_ref[pl.ds(r0, ROWS), :]
        ms = jnp.mean(xs * xs, axis=-1, keepdims=True)
        y = xs * lax.rsqrt(ms + EPS) * g_ref[...]
        dst_ref[pl.ds(r0, ROWS), :] = y.astype(dst_ref.dtype)
        return carry

    lax.fori_loop(0, tm // ROWS, body, 0, unroll=2)


def _sigmoid(v):
    return 0.5 * jnp.tanh(0.5 * v) + 0.5


def _prev_tokens(z, tail_prev, n):
    tm = z.shape[0]
    row = lax.broadcasted_iota(jnp.int32, (SUB, z.shape[1]), 0)
    wrapped = []
    for v in range(n):
        lo = tm - (n - v) * SUB
        cur = z[lo:lo + SUB]
        prev = tail_prev[v * SUB:(v + 1) * SUB]
        wrapped.append(pltpu.roll(jnp.where(row == SUB - 1, prev, cur), 1, 0))
    return [jnp.concatenate(wrapped[n - k:] + [z[:tm - k * SUB]], axis=0) for k in range(1, n + 1)]


def _mixer_kernel(x_ref, g1_ref, wlo_ref, whi_ref, wp_ref, bias_ref, caw_ref, cab_ref, wr_ref, wi_ref,
                  br_ref, bi_ref, lam_ref, cbw_ref, o_ref,
                  hn, sa, abf, bbf, carry_xa, carry_p, carry_h, *, tiles_per_seq, token_order_in):
    i = pl.program_id(0)
    t = pl.program_id(1)
    tm = x_ref.shape[0]
    f32 = jnp.float32
    bf16 = jnp.bfloat16
    col = pl.multiple_of((t % NB) * CB, CB)
    cs = pl.ds(col, CB)

    def in_phase(first):
        return jnp.logical_and(t >= first, t < first + NB)

    def dot(lhs_ref, w_ref):
        return jnp.dot(lhs_ref[...], w_ref[...], preferred_element_type=f32)

    def bias(seg):
        return bias_ref[seg:seg + 1, cs]

    @pl.when(jnp.logical_and(t == 0, i % tiles_per_seq == 0))
    def _():
        carry_xa[...] = jnp.zeros_like(carry_xa)
        carry_p[...] = jnp.zeros_like(carry_p)
        carry_h[...] = jnp.zeros_like(carry_h)

    def gate_cols(q):
        return pl.ds(pl.multiple_of(col + q * LRU_BLOCK_W, LRU_BLOCK_W), LRU_BLOCK_W)

    def gates_a(z):
        n = CONV_A_TAPS - 1
        z1, z2, z3 = _prev_tokens(z, carry_xa[:, cs], n)
        carry_xa[:, cs] = z[tm - n * SUB:]
        xc = (caw_ref[3:4, cs] * z + caw_ref[2:3, cs] * z1 + caw_ref[1:2, cs] * z2
              + caw_ref[0:1, cs] * z3 + cab_ref[:, cs])
        xcb = xc.astype(bf16)
        pre = []
        for q in range(CB // LRU_BLOCK_W):
            qs = slice(q * LRU_BLOCK_W, (q + 1) * LRU_BLOCK_W)
            gs = gate_cols(q)
            blk_id = (t % NB) * (CB // LRU_BLOCK_W) + q
            pre.append((
                jnp.dot(xcb[:, qs], wr_ref[blk_id], preferred_element_type=f32) + br_ref[:, gs],
                jnp.dot(xcb[:, qs], wi_ref[blk_id], preferred_element_type=f32) + bi_ref[:, gs]))
        return xc, pre

    def recurrence_a(xc, pre, ya):
        for q, (r_pre, i_pre) in enumerate(pre):
            qs = slice(q * LRU_BLOCK_W, (q + 1) * LRU_BLOCK_W)
            gs = gate_cols(q)
            half_c = (-0.5 * LRU_C * math.log2(math.e)) * jax.nn.softplus(-lam_ref[:, gs])
            a = jnp.exp2(half_c * jnp.tanh(0.5 * r_pre) + half_c)
            v = (1.0 - a) * (1.0 + a)
            mult = jnp.where(v > 0.0, v * lax.rsqrt(v), 0.0)
            b = mult * (_sigmoid(i_pre) * xc[:, qs])
            h = b[0:SUB]
            p = a[0:SUB]
            hs, ps = [h], [p]
            for u in range(1, tm // SUB):
                au = a[u * SUB:(u + 1) * SUB]
                h = au * h + b[u * SUB:(u + 1) * SUB]
                p = au * p
                hs.append(h)
                ps.append(p)
            c = carry_h[0:1, gs]
            enter = []
            for s in range(SUB):
                enter.append(c)
                c = h[s:s + 1] + p[s:s + 1] * c
            carry_h[0:1, gs] = c
            enter = jnp.concatenate(enter, axis=0)
            h_all = jnp.concatenate([hu + pu * enter for hu, pu in zip(hs, ps)], axis=0)
            abf[:, gs] = (h_all * jax.nn.gelu(ya[:, qs])).astype(bf16)

    def conv_b():
        p = (dot(hn, wlo_ref) + bias(S_CC)) * (dot(hn, whi_ref) + bias(S_CX))
        n = CONV_B_TAPS - 1
        p1, p2 = _prev_tokens(p, carry_p[:, cs], n)
        carry_p[:, cs] = p[tm - n * SUB:]
        sa[:, cs] = cbw_ref[2:3, cs] * p + cbw_ref[1:2, cs] * p1 + cbw_ref[0:1, cs] * p2

    @pl.when(t == 0)
    def _():
        _rmsnorm_rows(x_ref, g1_ref, hn, tm, interleave=token_order_in)

    @pl.when(in_phase(T_XAYA))
    def _():
        z = dot(hn, wlo_ref) + bias(S_XA)
        ya = dot(hn, whi_ref) + bias(S_YA)
        xc, pre = gates_a(z)
        recurrence_a(xc, pre, ya)

    @pl.when(in_phase(T_CCX))
    def _():
        conv_b()

    @pl.when(in_phase(T_PAGA))
    def _():
        bbf[:, cs] = ((dot(hn, wlo_ref) + bias(S_CB)) * sa[:, cs]).astype(bf16)
        sa[:, cs] = _sigmoid(dot(hn, whi_ref) + bias(S_GA)) * dot(abf, wp_ref)

    @pl.when(in_phase(T_PBGB))
    def _():
        gated_b = _sigmoid(dot(hn, whi_ref) + bias(S_GB)) * dot(bbf, wp_ref)
        abf[:, cs] = (sa[:, cs] + gated_b).astype(bf16)

    @pl.when(in_phase(T_WO))
    def _():
        if token_order_in:
            x_cols = jnp.concatenate(
                [_interleaved_rows(x_ref, g, cs) for g in range(tm // ROWS)], axis=0)
        else:
            x_cols = x_ref[:, cs]
        o_ref[...] = x_cols + dot(abf, wp_ref)


def _mlp_kernel(x_ref, g2_ref, w1_ref, w2_ref, gf_ref, o_ref, hn, ubf, *, n_steps, final_norm):
    j = pl.program_id(1)
    tm = x_ref.shape[0]
    f32 = jnp.float32

    @pl.when(j == 0)
    def _():
        _rmsnorm_rows(x_ref, g2_ref, hn, tm)
        o_ref[...] = x_ref[...]

    for k in range(WN // CB):
        cs = slice(k * CB, (k + 1) * CB)
        u = jnp.maximum(jnp.dot(hn[...], w1_ref[:, cs], preferred_element_type=f32), 0.0)
        ubf[:, cs] = (u * u).astype(ubf.dtype)
    for k in range(D // CB):
        cs = slice(k * CB, (k + 1) * CB)
        o_ref[:, cs] += jnp.dot(ubf[...], w2_ref[:, cs], preferred_element_type=f32)

    if final_norm:
        @pl.when(j == n_steps - 1)
        def _():
            _rmsnorm_rows(o_ref, gf_ref, o_ref, tm)


def _mixer(l, xf, g1, w_in, w_proj, b_in, caw, cab, wr, wi, br, bi, lam, cbw, *, tiles_per_seq,
           token_order_in):
    t = xf.shape[0]
    layer = lambda shape: pl.BlockSpec((None,) + shape, lambda i, j: (l,) + (0,) * len(shape))
    return pl.pallas_call(
        functools.partial(_mixer_kernel, tiles_per_seq=tiles_per_seq, token_order_in=token_order_in),
        grid=(t // TM, N_MIX),
        in_specs=[
            pl.BlockSpec((TM, D), lambda i, j: (i, 0), pipeline_mode=pl.Buffered(1)),
            layer((1, D)),
            pl.BlockSpec((None, D, CB), lambda i, j: (l, 0, _lo_block(j))),
            pl.BlockSpec((None, D, CB), lambda i, j: (l, 0, _hi_block(j))),
            pl.BlockSpec((None, D, CB), lambda i, j: (l,) + _proj_block(j)),
            layer(b_in.shape[1:]),
            layer(caw.shape[1:]),
            layer((1, D)),
            layer(wr.shape[1:]),
            layer(wi.shape[1:]),
            layer((1, D)),
            layer((1, D)),
            layer((1, D)),
            layer(cbw.shape[1:]),
        ],
        out_specs=pl.BlockSpec((TM, CB), lambda i, j: (i, jnp.maximum(j - T_WO, 0))),
        out_shape=jax.ShapeDtypeStruct((t, D), jnp.float32),
        scratch_shapes=[
            pltpu.VMEM((TM, D), jnp.bfloat16),
            pltpu.VMEM((TM, D), jnp.float32),
            pltpu.VMEM((TM, D), jnp.bfloat16),
            pltpu.VMEM((TM, D), jnp.bfloat16),
            pltpu.VMEM(((CONV_A_TAPS - 1) * SUB, D), jnp.float32),
            pltpu.VMEM(((CONV_B_TAPS - 1) * SUB, D), jnp.float32),
            pltpu.VMEM((SUB, D), jnp.float32),
        ],
        compiler_params=pltpu.CompilerParams(
            dimension_semantics=("arbitrary", "arbitrary"), vmem_limit_bytes=VMEM_LIMIT),
        name="mixer",
    )(xf, g1, w_in, w_in, w_proj, b_in, caw, cab, wr, wi, br, bi, lam, cbw)


def _mlp(l, xf, g2, w1, w2, gf, *, final_norm):
    t = xf.shape[0]
    n_steps = w1.shape[-1] // WN
    return pl.pallas_call(
        functools.partial(_mlp_kernel, n_steps=n_steps, final_norm=final_norm),
        grid=(t // TM, n_steps),
        in_specs=[
            pl.BlockSpec((TM, D), lambda i, j: (i, 0)),
            pl.BlockSpec((None, 1, D), lambda i, j: (l, 0, 0)),
            pl.BlockSpec((None, D, WN), lambda i, j: (l, 0, j)),
            pl.BlockSpec((None, WN, D), lambda i, j: (l, j, 0)),
            pl.BlockSpec((1, D), lambda i, j: (0, 0)),
        ],
        out_specs=pl.BlockSpec((TM, D), lambda i, j: (i, 0)),
        out_shape=jax.ShapeDtypeStruct((t, D), jnp.float32),
        scratch_shapes=[
            pltpu.VMEM((TM, D), jnp.bfloat16),
            pltpu.VMEM((TM, WN), jnp.bfloat16),
        ],
        compiler_params=pltpu.CompilerParams(
            dimension_semantics=("arbitrary", "arbitrary"), vmem_limit_bytes=VMEM_LIMIT),
        name="mlp",
    )(xf, g2, w1, w2, gf)


def kernel(x, norm1_g, w_in, b_in, conv_a_w, conv_a_b, lru_wr, lru_br, lru_wi, lru_bi, lru_lam, conv_b_w, w_pa, w_pb, w_o, norm2_g, w_mlp1, w_mlp2, final_g):
    bsz, seq, d = x.shape
    depth = w_in.shape[0]
    assert d == D and seq % TM == 0 and w_in.shape[-1] == 7 * D
    bf16 = jnp.bfloat16
    n_tiles = bsz * seq // TM
    xf = x.reshape(bsz * seq, d)
    w_in_b = w_in.astype(bf16)
    w_proj = jnp.concatenate([w_pa, w_pb, w_o], axis=1).astype(bf16)
    w1_b = w_mlp1.astype(bf16)
    w2_b = w_mlp2.astype(bf16)
    wr_b = lru_wr.astype(bf16)
    wi_b = lru_wi.astype(bf16)
    rows = lambda v: v.reshape(depth, 1, D)
    b_in3 = b_in.reshape(depth, 7, D)
    for l in range(depth):
        xf = _mixer(l, xf, rows(norm1_g), w_in_b, w_proj, b_in3, conv_a_w, rows(conv_a_b), wr_b, wi_b,
                    rows(lru_br.reshape(depth, D)), rows(lru_bi.reshape(depth, D)), rows(lru_lam),
                    conv_b_w, tiles_per_seq=seq // TM, token_order_in=(l == 0))
        xf = _mlp(l, xf, rows(norm2_g), w1_b, w2_b, final_g.reshape(1, D), final_norm=(l == depth - 1))
    return xf.reshape(n_tiles, SEG, SUB, d).transpose(0, 2, 1, 3).reshape(bsz, seq, d)
```

```python
import functools
import math

import jax
import jax.numpy as jnp
from jax import lax
from jax.experimental import pallas as pl
from jax.experimental.pallas import tpu as pltpu

D = 2048
LRU_BLOCK_W = 256
LRU_C = 8.0
EPS = 1e-6
TM = 1024
WN = 1024
CB = 512
ROWS = 64
SUB = 8
SEG = TM // SUB
CONV_A_TAPS = 4
CONV_B_TAPS = 3
VMEM_LIMIT = 60 * 1024 * 1024

S_XA, S_YA, S_CB, S_CC, S_CX, S_GA, S_GB = range(7)
R_PA, R_PB, R_WO = range(3)
NB = D // CB
T_XAYA, T_CCX, T_PAGA, T_PBGB, T_WO = (p * NB for p in range(5))
N_MIX = 5 * NB


def _phase_select(t, per_phase):
    out = per_phase[-1]
    for p in range(len(per_phase) - 2, -1, -1):
        out = jnp.where(t < (p + 1) * NB, per_phase[p], out)
    return out


def _lo_block(t):
    m = t % NB
    last = S_CB * NB + NB - 1
    return _phase_select(t, [S_XA * NB + m, S_CC * NB + m, S_CB * NB + m, last, last])


def _hi_block(t):
    m = t % NB
    return _phase_select(t, [S_YA * NB + m, S_CX * NB + m, S_GA * NB + m, S_GB * NB + m,
                             S_GB * NB + NB - 1])


def _proj_block(t):
    m = t % NB
    row = _phase_select(t, [R_PA, R_PA, R_PA, R_PB, R_WO])
    col = _phase_select(t, [0, 0, m, m, m])
    return row, col


def _interleaved_rows(src_ref, group, cols=slice(None)):
    seg = src_ref.shape[0] // SUB
    aligned = (lambda r: r) if isinstance(group, int) else (lambda r: pl.multiple_of(r, SUB))
    slabs = [src_ref[pl.ds(aligned(s * seg + group * SUB), SUB), cols] for s in range(SUB)]
    blk = jnp.swapaxes(jnp.stack(slabs, axis=0), 0, 1)
    return blk.reshape(SUB * SUB, blk.shape[-1])


def _rmsnorm_rows(src_ref, g_ref, dst_ref, tm, interleave=False, deinterleave=False):
    assert ROWS == SUB * SUB
    seg = tm // SUB

    def body(s, carry):
        r0 = pl.multiple_of(s * ROWS, ROWS)
        xs = _interleaved_rows(src_ref, s) if interleave else src_ref[pl.ds(r0, ROWS), :]
        ms = jnp.mean(xs * xs, axis=-1, keepdims=True)
        y = (xs * lax.rsqrt(ms + EPS) * g_ref[...]).astype(dst_ref.dtype)
        if deinterleave:
            blk = jnp.swapaxes(y.reshape(SUB, SUB, y.shape[-1]), 0, 1)
            for k in range(SUB):
                dst_ref[pl.ds(pl.multiple_of(k * seg + s * SUB, SUB), SUB), :] = blk[k]
        else:
            dst_ref[pl.ds(r0, ROWS), :] = y
        return carry

    lax.fori_loop(0, tm // ROWS, body, 0, unroll=2)


def _sigmoid(v):
    return 0.5 * jnp.tanh(0.5 * v) + 0.5


def _prev_tokens(z, tail_prev, n):
    tm = z.shape[0]
    row = lax.broadcasted_iota(jnp.int32, (SUB, z.shape[1]), 0)
    wrapped = []
    for v in range(n):
        lo = tm - (n - v) * SUB
        cur = z[lo:lo + SUB]
        prev = tail_prev[v * SUB:(v + 1) * SUB]
        wrapped.append(pltpu.roll(jnp.where(row == SUB - 1, prev, cur), 1, 0))
    return [jnp.concatenate(wrapped[n - k:] + [z[:tm - k * SUB]], axis=0) for k in range(1, n + 1)]


def _mixer_kernel(x_ref, g1_ref, wlo_ref, whi_ref, wp_ref, bias_ref, caw_ref, cab_ref, wr_ref, wi_ref,
                  br_ref, bi_ref, lam_ref, cbw_ref, o_ref,
                  hn, sa, abf, bbf, carry_xa, carry_p, carry_h, *, tiles_per_seq, token_order_in):
    i = pl.program_id(0)
    t = pl.program_id(1)
    tm = x_ref.shape[0]
    f32 = jnp.float32
    bf16 = jnp.bfloat16
    col = pl.multiple_of((t % NB) * CB, CB)
    cs = pl.ds(col, CB)

    def in_phase(first):
        return jnp.logical_and(t >= first, t < first + NB)

    def dot(lhs_ref, w_ref):
        return jnp.dot(lhs_ref[...], w_ref[...], preferred_element_type=f32)

    def bias(seg):
        return bias_ref[seg:seg + 1, cs]

    @pl.when(jnp.logical_and(t == 0, i % tiles_per_seq == 0))
    def _():
        carry_xa[...] = jnp.zeros_like(carry_xa)
        carry_p[...] = jnp.zeros_like(carry_p)
        carry_h[...] = jnp.zeros_like(carry_h)

    def gate_cols(q):
        return pl.ds(pl.multiple_of(col + q * LRU_BLOCK_W, LRU_BLOCK_W), LRU_BLOCK_W)

    def gates_a(z):
        n = CONV_A_TAPS - 1
        z1, z2, z3 = _prev_tokens(z, carry_xa[:, cs], n)
        carry_xa[:, cs] = z[tm - n * SUB:]
        xc = (caw_ref[3:4, cs] * z + caw_ref[2:3, cs] * z1 + caw_ref[1:2, cs] * z2
              + caw_ref[0:1, cs] * z3 + cab_ref[:, cs])
        xcb = xc.astype(bf16)
        pre = []
        for q in range(CB // LRU_BLOCK_W):
            qs = slice(q * LRU_BLOCK_W, (q + 1) * LRU_BLOCK_W)
            gs = gate_cols(q)
            blk_id = (t % NB) * (CB // LRU_BLOCK_W) + q
            pre.append((
                jnp.dot(xcb[:, qs], wr_ref[blk_id], preferred_element_type=f32) + br_ref[:, gs],
                jnp.dot(xcb[:, qs], wi_ref[blk_id], preferred_element_type=f32) + bi_ref[:, gs]))
        return xc, pre

    def recurrence_a(xc, pre, ya):
        for q, (r_pre, i_pre) in enumerate(pre):
            qs = slice(q * LRU_BLOCK_W, (q + 1) * LRU_BLOCK_W)
            gs = gate_cols(q)
            half_c = (-0.5 * LRU_C * math.log2(math.e)) * jax.nn.softplus(-lam_ref[:, gs])
            a = jnp.exp2(half_c * jnp.tanh(0.5 * r_pre) + half_c)
            v = (1.0 - a) * (1.0 + a)
            mult = jnp.where(v > 0.0, v * lax.rsqrt(v), 0.0)
            b = mult * (_sigmoid(i_pre) * xc[:, qs])
            h = b[0:SUB]
            p = a[0:SUB]
            hs, ps = [h], [p]
            for u in range(1, tm // SUB):
                au = a[u * SUB:(u + 1) * SUB]
                h = au * h + b[u * SUB:(u + 1) * SUB]
                p = au * p
                hs.append(h)
                ps.append(p)
            c = carry_h[0:1, gs]
            enter = []
            for s in range(SUB):
                enter.append(c)
                c = h[s:s + 1] + p[s:s + 1] * c
            carry_h[0:1, gs] = c
            enter = jnp.concatenate(enter, axis=0)
            h_all = jnp.concatenate([hu + pu * enter for hu, pu in zip(hs, ps)], axis=0)
            abf[:, gs] = (h_all * jax.nn.gelu(ya[:, qs])).astype(bf16)

    def conv_b():
        p = (dot(hn, wlo_ref) + bias(S_CC)) * (dot(hn, whi_ref) + bias(S_CX))
        n = CONV_B_TAPS - 1
        p1, p2 = _prev_tokens(p, carry_p[:, cs], n)
        carry_p[:, cs] = p[tm - n * SUB:]
        sa[:, cs] = cbw_ref[2:3, cs] * p + cbw_ref[1:2, cs] * p1 + cbw_ref[0:1, cs] * p2

    @pl.when(t == 0)
    def _():
        _rmsnorm_rows(x_ref, g1_ref, hn, tm, interleave=token_order_in)

    @pl.when(in_phase(T_XAYA))
    def _():
        z = dot(hn, wlo_ref) + bias(S_XA)
        ya = dot(hn, whi_ref) + bias(S_YA)
        xc, pre = gates_a(z)
        recurrence_a(xc, pre, ya)

    @pl.when(in_phase(T_CCX))
    def _():
        conv_b()

    @pl.when(in_phase(T_PAGA))
    def _():
        bbf[:, cs] = ((dot(hn, wlo_ref) + bias(S_CB)) * sa[:, cs]).astype(bf16)
        sa[:, cs] = _sigmoid(dot(hn, whi_ref) + bias(S_GA)) * dot(abf, wp_ref)

    @pl.when(in_phase(T_PBGB))
    def _():
        gated_b = _sigmoid(dot(hn, whi_ref) + bias(S_GB)) * dot(bbf, wp_ref)
        abf[:, cs] = (sa[:, cs] + gated_b).astype(bf16)

    @pl.when(in_phase(T_WO))
    def _():
        if token_order_in:
            x_cols = jnp.concatenate(
                [_interleaved_rows(x_ref, g, cs) for g in range(tm // ROWS)], axis=0)
        else:
            x_cols = x_ref[:, cs]
        o_ref[...] = x_cols + dot(abf, wp_ref)


def _mlp_kernel(x_ref, g2_ref, w1_ref, w2_ref, gf_ref, o_ref, hn, ubf, *maybe_acc, n_steps, final_norm):
    j = pl.program_id(1)
    tm = x_ref.shape[0]
    f32 = jnp.float32
    acc = maybe_acc[0] if final_norm else o_ref

    @pl.when(j == 0)
    def _():
        _rmsnorm_rows(x_ref, g2_ref, hn, tm)
        acc[...] = x_ref[...]

    for k in range(WN // CB):
        cs = slice(k * CB, (k + 1) * CB)
        u = jnp.maximum(jnp.dot(hn[...], w1_ref[:, cs], preferred_element_type=f32), 0.0)
        ubf[:, cs] = (u * u).astype(ubf.dtype)
    for k in range(D // CB):
        cs = slice(k * CB, (k + 1) * CB)
        acc[:, cs] += jnp.dot(ubf[...], w2_ref[:, cs], preferred_element_type=f32)

    if final_norm:
        @pl.when(j == n_steps - 1)
        def _():
            _rmsnorm_rows(acc, gf_ref, o_ref, tm, deinterleave=True)


def _mixer(l, xf, g1, w_in, w_proj, b_in, caw, cab, wr, wi, br, bi, lam, cbw, *, tiles_per_seq,
           token_order_in):
    t = xf.shape[0]
    layer = lambda shape: pl.BlockSpec((None,) + shape, lambda i, j: (l,) + (0,) * len(shape))
    return pl.pallas_call(
        functools.partial(_mixer_kernel, tiles_per_seq=tiles_per_seq, token_order_in=token_order_in),
        grid=(t // TM, N_MIX),
        in_specs=[
            pl.BlockSpec((TM, D), lambda i, j: (i, 0), pipeline_mode=pl.Buffered(1)),
            layer((1, D)),
            pl.BlockSpec((None, D, CB), lambda i, j: (l, 0, _lo_block(j))),
            pl.BlockSpec((None, D, CB), lambda i, j: (l, 0, _hi_block(j))),
            pl.BlockSpec((None, D, CB), lambda i, j: (l,) + _proj_block(j)),
            layer(b_in.shape[1:]),
            layer(caw.shape[1:]),
            layer((1, D)),
            layer(wr.shape[1:]),
            layer(wi.shape[1:]),
            layer((1, D)),
            layer((1, D)),
            layer((1, D)),
            layer(cbw.shape[1:]),
        ],
        out_specs=pl.BlockSpec((TM, CB), lambda i, j: (i, jnp.maximum(j - T_WO, 0))),
        out_shape=jax.ShapeDtypeStruct((t, D), jnp.float32),
        scratch_shapes=[
            pltpu.VMEM((TM, D), jnp.bfloat16),
            pltpu.VMEM((TM, D), jnp.float32),
            pltpu.VMEM((TM, D), jnp.bfloat16),
            pltpu.VMEM((TM, D), jnp.bfloat16),
            pltpu.VMEM(((CONV_A_TAPS - 1) * SUB, D), jnp.float32),
            pltpu.VMEM(((CONV_B_TAPS - 1) * SUB, D), jnp.float32),
            pltpu.VMEM((SUB, D), jnp.float32),
        ],
        compiler_params=pltpu.CompilerParams(
            dimension_semantics=("arbitrary", "arbitrary"), vmem_limit_bytes=VMEM_LIMIT),
        name="mixer",
    )(xf, g1, w_in, w_in, w_proj, b_in, caw, cab, wr, wi, br, bi, lam, cbw)


def _mlp(l, xf, g2, w1, w2, gf, *, final_norm):
    t = xf.shape[0]
    n_steps = w1.shape[-1] // WN
    return pl.pallas_call(
        functools.partial(_mlp_kernel, n_steps=n_steps, final_norm=final_norm),
        grid=(t // TM, n_steps),
        in_specs=[
            pl.BlockSpec((TM, D), lambda i, j: (i, 0)),
            pl.BlockSpec((None, 1, D), lambda i, j: (l, 0, 0)),
            pl.BlockSpec((None, D, WN), lambda i, j: (l, 0, j)),
            pl.BlockSpec((None, WN, D), lambda i, j: (l, j, 0)),
            pl.BlockSpec((1, D), lambda i, j: (0, 0)),
        ],
        out_specs=pl.BlockSpec((TM, D), lambda i, j: (i, 0),
                               **({"pipeline_mode": pl.Buffered(1)} if final_norm else {})),
        out_shape=jax.ShapeDtypeStruct((t, D), jnp.float32),
        scratch_shapes=[
            pltpu.VMEM((TM, D), jnp.bfloat16),
            pltpu.VMEM((TM, WN), jnp.bfloat16),
        ] + ([pltpu.VMEM((TM, D), jnp.float32)] if final_norm else []),
        compiler_params=pltpu.CompilerParams(
            dimension_semantics=("arbitrary", "arbitrary"), vmem_limit_bytes=VMEM_LIMIT),
        name="mlp",
    )(xf, g2, w1, w2, gf)


def kernel(x, norm1_g, w_in, b_in, conv_a_w, conv_a_b, lru_wr, lru_br, lru_wi, lru_bi, lru_lam, conv_b_w, w_pa, w_pb, w_o, norm2_g, w_mlp1, w_mlp2, final_g):
    bsz, seq, d = x.shape
    depth = w_in.shape[0]
    assert d == D and seq % TM == 0 and w_in.shape[-1] == 7 * D
    bf16 = jnp.bfloat16
    xf = x.reshape(bsz * seq, d)
    w_in_b = w_in.astype(bf16)
    w_proj = jnp.concatenate([w_pa, w_pb, w_o], axis=1).astype(bf16)
    w1_b = w_mlp1.astype(bf16)
    w2_b = w_mlp2.astype(bf16)
    wr_b = lru_wr.astype(bf16)
    wi_b = lru_wi.astype(bf16)
    rows = lambda v: v.reshape(depth, 1, D)
    b_in3 = b_in.reshape(depth, 7, D)
    for l in range(depth):
        xf = _mixer(l, xf, rows(norm1_g), w_in_b, w_proj, b_in3, conv_a_w, rows(conv_a_b), wr_b, wi_b,
                    rows(lru_br.reshape(depth, D)), rows(lru_bi.reshape(depth, D)), rows(lru_lam),
                    conv_b_w, tiles_per_seq=seq // TM, token_order_in=(l == 0))
        xf = _mlp(l, xf, rows(norm2_g), w1_b, w2_b, final_g.reshape(1, D), final_norm=(l == depth - 1))
    return xf.reshape(bsz, seq, d)
```
